```python
import math
import jax, jax.numpy as jnp
from jax import lax
import numpy as np

D_MODEL = 1024
BATCH = 32
SEQ = 2048
DEPTH = 4

CHUNK = 64
Q_BLOCK = 128
MEM_LEN = 256
EPS = 1e-6

MIX_WIDTH = D_MODEL
SB_HEAD_DIM = 64
SB_WIDTH = MIX_WIDTH // 2
SB_HEADS = SB_WIDTH // SB_HEAD_DIM
SSD_HEAD_DIM = 64
SSD_WIDTH = MIX_WIDTH - SB_WIDTH
SSD_HEADS = SSD_WIDTH // SSD_HEAD_DIM
SSD_GROUPS = 2
SSD_STATE = 128
CONV_WIDTH = 4
CONV_CH = SSD_WIDTH + 2 * SSD_GROUPS * SSD_STATE
DT_MIN = 1e-3
DT_MAX = 1e-1
IN_PROJ = 3 * SB_WIDTH + SSD_WIDTH + CONV_CH + SSD_HEADS
IN_SPLITS = (SB_WIDTH, 2 * SB_WIDTH, 3 * SB_WIDTH, 3 * SB_WIDTH + SSD_WIDTH,
             3 * SB_WIDTH + SSD_WIDTH + CONV_CH)
X_HEADS = 4
X_HEAD_DIM = D_MODEL // X_HEADS
D_FF = 256 * (-(-8 * D_MODEL // 3 // 256))
N_EXPERTS = 8
TOP_K = 2
EXPERT_FF = 7 * D_MODEL // 2
MOE_BLOCK = 256
N_DENSE = (DEPTH + 1) // 2
N_MOE = DEPTH // 2

kernel_name = "hymba_stickbreak_ssd_moe_trunk"


def rms_norm(x, g):
    xf = x.astype(jnp.float32)
    y = xf * lax.rsqrt(jnp.mean(xf * xf, axis=-1, keepdims=True) + EPS)
    return (y * g.astype(jnp.float32)).astype(x.dtype)


def stick_breaking_attention(q, k, v):
    b, s, h, d = q.shape
    nb = s // Q_BLOCK
    scale = d ** -0.5
    qh = q.transpose(0, 2, 1, 3).reshape(b, h, nb, Q_BLOCK, d).transpose(2, 0, 1, 3, 4)
    kh = k.transpose(0, 2, 1, 3)
    vh = v.transpose(0, 2, 1, 3)
    key_pos = jnp.arange(s, dtype=jnp.int32)

    def block(args):
        q_blk, t0 = args
        z = jnp.einsum('bhqd,bhkd->bhqk', q_blk, kh,
                       preferred_element_type=jnp.float32) * scale
        q_pos = t0 + jnp.arange(Q_BLOCK, dtype=jnp.int32)
        before = key_pos[None, :] < q_pos[:, None]
        log_fail = jnp.where(before, jax.nn.log_sigmoid(-z), 0.0)
        log_surv = lax.cumsum(log_fail, axis=3, reverse=True) - log_fail
        w = jnp.where(before, jnp.exp(jax.nn.log_sigmoid(z) + log_surv), 0.0)
        return jnp.einsum('bhqk,bhkd->bhqd', w.astype(vh.dtype), vh)

    o = lax.map(block, (qh, jnp.arange(nb, dtype=jnp.int32) * Q_BLOCK))
    return o.transpose(1, 0, 3, 2, 4).reshape(b, s, h * d)


def causal_depthwise_conv(x, w, bias):
    c = x.shape[-1]
    y = lax.conv_general_dilated(
        x, w[:, None, :].astype(x.dtype), window_strides=(1,),
        padding=[(CONV_WIDTH - 1, 0)], dimension_numbers=('NWC', 'WIO', 'NWC'),
        feature_group_count=c)
    return y + bias.astype(x.dtype)


def ssd_scan(xs, dt, a, bmat, cmat):
    b, s, h, p = xs.shape
    g, n = bmat.shape[2], bmat.shape[3]
    r = h // g
    nc = s // CHUNK
    f32 = jnp.float32
    x = xs.astype(f32).reshape(b, nc, CHUNK, g, r, p)
    dtc = dt.reshape(b, nc, CHUNK, g, r)
    B = bmat.astype(f32).reshape(b, nc, CHUNK, g, n)
    C = cmat.astype(f32).reshape(b, nc, CHUNK, g, n)
    xdt = x * dtc[..., None]
    la = jnp.moveaxis(jnp.cumsum(dtc * a.reshape(g, r), axis=2), 2, -1)
    causal = jnp.tril(jnp.ones((CHUNK, CHUNK), dtype=bool))
    seg = jnp.exp(jnp.where(causal, la[..., :, None] - la[..., None, :], -jnp.inf))
    cb = jnp.einsum('bclgn,bcsgn->bcgls', C, B)
    y_diag = jnp.einsum('bcgls,bcgrls,bcsgrp->bclgrp', cb, seg, xdt)
    decay_end = jnp.exp(la[..., -1:] - la)
    states = jnp.einsum('bclgn,bcgrl,bclgrp->bcgrpn', B, decay_end, xdt)
    chunk_decay = jnp.exp(la[..., -1])

    def step(state, inp):
        st, dec = inp
        return dec[..., None, None] * state + st, state

    h0 = jnp.zeros((b, g, r, p, n), f32)
    _, h_prev = lax.scan(step, h0, (jnp.moveaxis(states, 1, 0), jnp.moveaxis(chunk_decay, 1, 0)))
    h_prev = jnp.moveaxis(h_prev, 0, 1)
    y_off = jnp.einsum('bclgn,bcgrpn,bcgrl->bclgrp', C, h_prev, jnp.exp(la))
    return (y_diag + y_off).reshape(b, s, h * p)


def ssd_mixer(z, xbc, dt_raw, conv_w, conv_b, dt_bias, a_log, d_skip, norm_g):
    b, s, _ = xbc.shape
    xbc = jax.nn.silu(causal_depthwise_conv(xbc, conv_w, conv_b))
    gn = SSD_GROUPS * SSD_STATE
    xs = xbc[..., :SSD_WIDTH].reshape(b, s, SSD_HEADS, SSD_HEAD_DIM)
    bm = xbc[..., SSD_WIDTH:SSD_WIDTH + gn].reshape(b, s, SSD_GROUPS, SSD_STATE)
    cm = xbc[..., SSD_WIDTH + gn:].reshape(b, s, SSD_GROUPS, SSD_STATE)
    dt = jax.nn.softplus(dt_raw.astype(jnp.float32) + dt_bias.astype(jnp.float32))
    a = -jnp.exp(a_log.astype(jnp.float32))
    y = ssd_scan(xs, dt, a, bm, cm)
    y = y + (xs.astype(jnp.float32) * d_skip.astype(jnp.float32)[:, None]).reshape(b, s, SSD_WIDTH)
    y = y * jax.nn.silu(z.astype(jnp.float32))
    return rms_norm(y, norm_g).astype(z.dtype)


def memory_cross_attention(h, mem_n, wq, wk, wv, wo):
    b, s, _ = h.shape
    m = mem_n.shape[1]
    q = (h @ wq).reshape(b, s, X_HEADS, X_HEAD_DIM)
    k = (mem_n @ wk).reshape(b, m, X_HEADS, X_HEAD_DIM)
    v = (mem_n @ wv).reshape(b, m, X_HEADS, X_HEAD_DIM)
    sc = jnp.einsum('bshd,bmhd->bhsm', q, k, preferred_element_type=jnp.float32) * (X_HEAD_DIM ** -0.5)
    pr = jax.nn.softmax(sc, axis=-1)
    o = jnp.einsum('bhsm,bmhd->bshd', pr.astype(v.dtype), v).reshape(b, s, D_MODEL)
    return o @ wo


def swiglu(h, w1, w3, w2):
    return (jax.nn.silu(h @ w1) * (h @ w3)) @ w2


def moe_swiglu(h, w_router, w1, w3, w2):
    b, s, d = h.shape
    n = b * s
    hf = h.reshape(n, d)
    logits = jnp.dot(hf, w_router, preferred_element_type=jnp.float32)
    top_logit, top_e = lax.top_k(logits, TOP_K)
    gates = jax.nn.softmax(top_logit, axis=-1)
    e_flat = top_e.reshape(-1).astype(jnp.int32)
    g_flat = gates.reshape(-1)
    m = n * TOP_K
    tok_flat = jnp.arange(m, dtype=jnp.int32) // TOP_K
    order = jnp.argsort(e_flat)
    e_sorted = e_flat[order]
    counts = jnp.zeros((N_EXPERTS,), jnp.int32).at[e_flat].add(1)
    padded = (counts + MOE_BLOCK - 1) // MOE_BLOCK * MOE_BLOCK
    starts = jnp.cumsum(counts) - counts
    pends = jnp.cumsum(padded)
    pstarts = pends - padded
    dest = pstarts[e_sorted] + (jnp.arange(m, dtype=jnp.int32) - starts[e_sorted])
    cap = (-(-m // MOE_BLOCK)) * MOE_BLOCK + N_EXPERTS * MOE_BLOCK
    n_blocks = cap // MOE_BLOCK
    buf_tok = jnp.zeros((cap,), jnp.int32).at[dest].set(tok_flat[order])
    buf_gate = jnp.zeros((cap,), jnp.float32).at[dest].set(g_flat[order])
    block_start = jnp.arange(n_blocks, dtype=jnp.int32) * MOE_BLOCK
    block_e = jnp.minimum(jnp.sum((block_start[:, None] >= pends[None, :]).astype(jnp.int32), axis=1),
                          N_EXPERTS - 1)
    xb = hf[buf_tok].reshape(n_blocks, MOE_BLOCK, d)

    def expert_block(args):
        x_blk, e = args
        return swiglu(x_blk, w1[e], w3[e], w2[e])

    yb = lax.map(expert_block, (xb, block_e)).reshape(cap, d)
    yb = yb * buf_gate[:, None].astype(yb.dtype)
    out = jnp.zeros((n, d), h.dtype).at[buf_tok].add(yb)
    return out.reshape(b, s, d)


def _normal(key, shape, fan_in):
    return jax.random.normal(key, shape, jnp.float32) * (fan_in ** -0.5)


def _gain(key, shape):
    return 1.0 + 0.02 * jax.random.normal(key, shape, jnp.float32)


def setup_inputs(seed: int = 0) -> dict:
    key = jax.random.key(seed)
    ks = jax.random.split(key, 32)
    L = DEPTH
    dt0 = jnp.exp(jax.random.uniform(ks[6], (L, SSD_HEADS), jnp.float32,
                                     math.log(DT_MIN), math.log(DT_MAX)))
    dt_bias = dt0 + jnp.log(-jnp.expm1(-dt0))
    return {
        "x": jax.random.normal(ks[0], (BATCH, SEQ, D_MODEL), jnp.float32),
        "mem": jax.random.normal(ks[1], (BATCH, MEM_LEN, D_MODEL), jnp.float32),
        "mem_norm": _gain(ks[2], (D_MODEL,)),
        "norm_mix": _gain(ks[3], (L, D_MODEL)),
        "w_in": _normal(ks[4], (L, D_MODEL, IN_PROJ), D_MODEL),
        "conv_w": _normal(ks[5], (L, CONV_WIDTH, CONV_CH), CONV_WIDTH),
        "conv_b": 0.02 * jax.random.normal(ks[7], (L, CONV_CH), jnp.float32),
        "dt_bias": dt_bias,
        "a_log": jnp.log(jax.random.uniform(ks[8], (L, SSD_HEADS), jnp.float32, 1.0, 16.0)),
        "d_skip": 1.0 + 0.1 * jax.random.normal(ks[9], (L, SSD_HEADS), jnp.float32),
        "sb_norm": _gain(ks[10], (L, SB_WIDTH)),
        "ssd_norm": _gain(ks[11], (L, SSD_WIDTH)),
        "w_out": _normal(ks[12], (L, MIX_WIDTH, D_MODEL), MIX_WIDTH),
        "norm_cross": _gain(ks[13], (L, D_MODEL)),
        "wq_x": _normal(ks[14], (L, D_MODEL, D_MODEL), D_MODEL),
        "wk_x": _normal(ks[15], (L, D_MODEL, D_MODEL), D_MODEL),
        "wv_x": _normal(ks[16], (L, D_MODEL, D_MODEL), D_MODEL),
        "wo_x": _normal(ks[17], (L, D_MODEL, D_MODEL), D_MODEL),
        "norm_ffn": _gain(ks[18], (L, D_MODEL)),
        "ffn_w1": _normal(ks[19], (N_DENSE, D_MODEL, D_FF), D_MODEL),
        "ffn_w3": _normal(ks[20], (N_DENSE, D_MODEL, D_FF), D_MODEL),
        "ffn_w2": _normal(ks[21], (N_DENSE, D_FF, D_MODEL), D_FF),
        "router": _normal(ks[22], (N_MOE, D_MODEL, N_EXPERTS), D_MODEL),
        "moe_w1": _normal(ks[23], (N_MOE, N_EXPERTS, D_MODEL, EXPERT_FF), D_MODEL),
        "moe_w3": _normal(ks[24], (N_MOE, N_EXPERTS, D_MODEL, EXPERT_FF), D_MODEL),
        "moe_w2": _normal(ks[25], (N_MOE, N_EXPERTS, EXPERT_FF, D_MODEL), EXPERT_FF),
        "final_norm": _gain(ks[26], (D_MODEL,)),
    }


def reference(x, mem, mem_norm, norm_mix, w_in, conv_w, conv_b, dt_bias, a_log, d_skip,
              sb_norm, ssd_norm, w_out, norm_cross, wq_x, wk_x, wv_x, wo_x, norm_ffn,
              ffn_w1, ffn_w3, ffn_w2, router, moe_w1, moe_w3, moe_w2, final_norm):
    b, s, _ = x.shape
    mem_n = rms_norm(mem, mem_norm)
    h = x
    for i in range(DEPTH):
        u = rms_norm(h, norm_mix[i])
        proj = u @ w_in[i]
        q, k, v, z, xbc, dt_raw = jnp.split(proj, IN_SPLITS, axis=-1)
        shp = (b, s, SB_HEADS, SB_HEAD_DIM)
        attn = stick_breaking_attention(q.reshape(shp), k.reshape(shp), v.reshape(shp))
        attn = rms_norm(attn, sb_norm[i])
        ssd = ssd_mixer(z, xbc, dt_raw, conv_w[i], conv_b[i], dt_bias[i], a_log[i],
                        d_skip[i], ssd_norm[i])
        h = h + jnp.concatenate([attn, ssd], axis=-1) @ w_out[i]
        h = h + memory_cross_attention(rms_norm(h, norm_cross[i]), mem_n,
                                       wq_x[i], wk_x[i], wv_x[i], wo_x[i])
        u = rms_norm(h, norm_ffn[i])
        if i % 2 == 0:
            j = i // 2
            h = h + swiglu(u, ffn_w1[j], ffn_w3[j], ffn_w2[j])
        else:
            j = i // 2
            h = h + moe_swiglu(u, router[j], moe_w1[j], moe_w3[j], moe_w2[j])
    return rms_norm(h, final_norm)
```

```python
import functools

import jax
import jax.numpy as jnp
from jax import lax
from jax.experimental import pallas as pl
from jax.experimental.pallas import tpu as pltpu

F32 = jnp.float32
BF16 = jnp.bfloat16

EPS = 1e-6
HEAD_DIM = 64
SB_WIDTH = 512
SSD_WIDTH = 512
SSD_HEADS = 8
SSD_STATE = 128
SSD_GROUPS = 2
CONV_WIDTH = 4
X_HEADS = 4
N_EXPERTS = 8
LANES = 128
SUBLANES = 8
VMEM_LIMIT_BYTES = 56 * 1024 * 1024
UNDERFLOW_LOG = -105.0

SB_BLOCK = 128
SSD_CHUNK = 256
ROW_TILE = 512
MOE_TILE = 512


def _params(*sem):
    return pltpu.CompilerParams(dimension_semantics=sem, vmem_limit_bytes=VMEM_LIMIT_BYTES)


def _rms(x, g):
    return x * lax.rsqrt(jnp.mean(x * x, axis=-1, keepdims=True) + EPS) * g


def _silu(x):
    return x / (1.0 + jnp.exp(-x))


def _dot(a, b):
    return jnp.dot(a, b, preferred_element_type=F32)


def _dot_nt(a, b):
    return lax.dot_general(a, b, (((1,), (1,)), ((), ())), preferred_element_type=F32)


def _dot_tn(a, b):
    return lax.dot_general(a, b, (((0,), (0,)), ((), ())), preferred_element_type=F32)


def _split2(x):
    hi = x.astype(BF16)
    lo = (x - hi.astype(F32)).astype(BF16)
    return hi, lo


def _split3(x):
    hi = x.astype(BF16)
    r = x - hi.astype(F32)
    mid = r.astype(BF16)
    lo = (r - mid.astype(F32)).astype(BF16)
    return hi, mid, lo


def _norm_proj_kernel(x_ref, g_ref, w_ref, *rest, col_chunk, with_aux):
    if with_aux:
        waux_ref, o_ref, oaux_ref = rest
    else:
        (o_ref,) = rest
    u = _rms(x_ref[...], g_ref[...]).astype(BF16)
    for c in range(o_ref.shape[1] // col_chunk):
        sl = slice(c * col_chunk, (c + 1) * col_chunk)
        o_ref[:, sl] = _dot(u, w_ref[:, sl]).astype(o_ref.dtype)
    if with_aux:
        oaux_ref[...] = _dot(u, waux_ref[...])


def _norm_proj(x, g, w, w_aux=None, *, tm):
    n, d = x.shape
    m = w.shape[1]
    with_aux = w_aux is not None
    in_specs = [
        pl.BlockSpec((tm, d), lambda i: (i, 0)),
        pl.BlockSpec((1, d), lambda i: (0, 0)),
        pl.BlockSpec((d, m), lambda i: (0, 0)),
    ]
    out_shape = [jax.ShapeDtypeStruct((n, m), BF16)]
    out_specs = [pl.BlockSpec((tm, m), lambda i: (i, 0))]
    args = [x, g, w]
    if with_aux:
        in_specs.append(pl.BlockSpec((d, LANES), lambda i: (0, 0)))
        out_shape.append(jax.ShapeDtypeStruct((n, LANES), F32))
        out_specs.append(pl.BlockSpec((tm, LANES), lambda i: (i, 0)))
        args.append(w_aux)
    res = pl.pallas_call(
        functools.partial(_norm_proj_kernel, col_chunk=512, with_aux=with_aux),
        grid=(n // tm,),
        in_specs=in_specs,
        out_specs=out_specs,
        out_shape=out_shape,
        compiler_params=_params("parallel"),
        name="norm_proj_aux" if with_aux else "norm_proj",
    )(*args)
    return res if with_aux else res[0]


def _sb_kernel(q_ref, k_ref, v_ref, o_ref, *, blk, scale):
    nblk = q_ref.shape[0] // blk
    rows = lax.broadcasted_iota(jnp.int32, (blk, blk), 0)
    cols = lax.broadcasted_iota(jnp.int32, (blk, blk), 1)
    suffix_ones = (rows >= cols).astype(BF16)
    before = cols < rows

    def sweep(i, hh):
        lanes = slice(hh * HEAD_DIM, (hh + 1) * HEAD_DIM)
        q = q_ref[pl.ds(pl.multiple_of(i * blk, blk), blk), lanes]

        def key_block(kb, carry, acc, diag):
            ks = pl.ds(pl.multiple_of(kb * blk, blk), blk)
            z = _dot_nt(q, k_ref[ks, lanes]) * scale
            lf = -(jnp.maximum(z, 0.0) + jnp.log(1.0 + jnp.exp(-jnp.abs(z))))
            if diag:
                lf = jnp.where(before, lf, 0.0)
            hi, lo = _split2(lf)
            suf = _dot(hi, suffix_ones) + _dot(lo, suffix_ones)
            w = jnp.exp(z + suf + carry)
            if diag:
                w = jnp.where(before, w, 0.0)
            acc = acc + _dot(w.astype(BF16), v_ref[ks, lanes])
            return carry + suf[:, 0:1], acc

        carry, acc = key_block(i, jnp.zeros((blk, 1), F32), jnp.zeros((blk, HEAD_DIM), F32), True)

        def body(step, state):
            return key_block(i - 1 - step, state[0], state[1], False)

        carry, acc = lax.fori_loop(0, i, body, (carry, acc))
        return acc

    def q_block(i, _):
        out = jnp.concatenate([sweep(i, 0), sweep(i, 1)], axis=-1)
        o_ref[pl.ds(pl.multiple_of(i * blk, blk), blk), :] = out.astype(o_ref.dtype)
        return 0

    lax.fori_loop(0, nblk, q_block, 0)


def _sb_attention(proj3, *, blk):
    b, s, _ = proj3.shape
    pairs = SB_WIDTH // LANES
    spec = lambda off: pl.BlockSpec((None, s, LANES), lambda bi, hp: (bi, 0, off + hp))
    return pl.pallas_call(
        functools.partial(_sb_kernel, blk=blk, scale=HEAD_DIM ** -0.5),
        grid=(b, pairs),
        in_specs=[spec(0), spec(pairs), spec(2 * pairs)],
        out_specs=pl.BlockSpec((None, s, LANES), lambda bi, hp: (bi, 0, hp)),
        out_shape=jax.ShapeDtypeStruct((b, s, SB_WIDTH), F32),
        compiler_params=_params("parallel", "parallel"),
        name="stick_breaking",
    )(proj3, proj3, proj3)


def _ssd_kernel(xbc_ref, z_ref, dt_ref, cw_ref, cb_ref, dtb_ref, alog_ref, dskip_ref, expand_ref,
                o_ref, xpad_ref, state_ref, *, chunk):
    c = pl.program_id(1)
    gn = SSD_GROUPS * SSD_STATE
    half = SSD_WIDTH // SSD_GROUPS

    @pl.when(c == 0)
    def _():
        xpad_ref[0:SUBLANES, :] = jnp.zeros((SUBLANES, xpad_ref.shape[1]), F32)
        state_ref[...] = jnp.zeros(state_ref.shape, F32)

    @pl.when(c > 0)
    def _():
        xpad_ref[0:SUBLANES, :] = xpad_ref[chunk:chunk + SUBLANES, :]

    xpad_ref[SUBLANES:SUBLANES + chunk, :] = xbc_ref[...].astype(F32)
    conv = cb_ref[...]
    for tap in range(CONV_WIDTH):
        start = SUBLANES - (CONV_WIDTH - 1) + tap
        conv = conv + cw_ref[tap:tap + 1, :] * xpad_ref[start:start + chunk, :]
    act = _silu(conv)
    xs = act[:, :SSD_WIDTH]
    bm = act[:, SSD_WIDTH:SSD_WIDTH + gn].astype(BF16)
    cm = act[:, SSD_WIDTH + gn:].astype(BF16)

    raw = dt_ref[...] + dtb_ref[...]
    dt8 = jnp.maximum(raw, 0.0) + jnp.log(1.0 + jnp.exp(-jnp.abs(raw)))
    expand = expand_ref[...]
    d1, d2, d3 = _split3(dt8)
    dt = _dot(d1, expand) + _dot(d2, expand) + _dot(d3, expand)
    a = -jnp.exp(alog_ref[...])
    rows = lax.broadcasted_iota(jnp.int32, (chunk, chunk), 0)
    cols = lax.broadcasted_iota(jnp.int32, (chunk, chunk), 1)
    causal = rows >= cols
    prefix_ones = causal.astype(BF16)
    l1, l2, l3 = _split3(dt * a)
    la = _dot(prefix_ones, l1) + _dot(prefix_ones, l2) + _dot(prefix_ones, l3)
    la_t = jnp.transpose(la)
    xdt = xs * dt
    xdt_b = xdt.astype(BF16)

    y_parts = []
    for g in range(SSD_GROUPS):
        cb = _dot_nt(cm[:, g * SSD_STATE:(g + 1) * SSD_STATE], bm[:, g * SSD_STATE:(g + 1) * SSD_STATE])
        for r in range(SSD_HEADS // SSD_GROUPS):
            lane0 = (g * (SSD_HEADS // SSD_GROUPS) + r) * HEAD_DIM
            diff = la[:, lane0:lane0 + 1] - la_t[lane0:lane0 + 1, :]
            m = cb * jnp.exp(jnp.where(causal, diff, -jnp.inf))
            y_parts.append(_dot(m.astype(BF16), xdt_b[:, lane0:lane0 + HEAD_DIM]))
    y = jnp.concatenate(y_parts, axis=-1)

    state = state_ref[...]
    state_b = state.astype(BF16)
    y_off = jnp.concatenate(
        [_dot(cm[:, g * SSD_STATE:(g + 1) * SSD_STATE], state_b[:, g * half:(g + 1) * half])
         for g in range(SSD_GROUPS)], axis=-1)
    y = y + y_off * jnp.exp(la)
    la_last = la[chunk - 1:chunk, :]
    xdec = (xdt * jnp.exp(la_last - la)).astype(BF16)
    upd = jnp.concatenate(
        [_dot_tn(bm[:, g * SSD_STATE:(g + 1) * SSD_STATE], xdec[:, g * half:(g + 1) * half])
         for g in range(SSD_GROUPS)], axis=-1)
    state_ref[...] = state * jnp.exp(la_last) + upd

    y = y + xs * dskip_ref[...]
    o_ref[...] = y * _silu(z_ref[...].astype(F32))


def _ssd(proj3, dt3, conv_w, conv_b, dt_bias, a_log, d_skip, *, chunk):
    b, s, _ = proj3.shape
    conv_ch = conv_w.shape[1]
    rep = lambda v: jnp.repeat(v.astype(F32), HEAD_DIM)[None, :]
    dtb = jnp.zeros((1, LANES), F32).at[0, :SSD_HEADS].set(dt_bias.astype(F32))
    expand = (jnp.arange(LANES)[:, None] == (jnp.arange(SSD_WIDTH) // HEAD_DIM)[None, :]).astype(BF16)
    z_blk = (3 * SB_WIDTH) // SSD_WIDTH
    xbc_blk = (3 * SB_WIDTH + SSD_WIDTH) // conv_ch
    const = lambda shape: pl.BlockSpec(shape, lambda bi, ci: (0,) * len(shape))
    return pl.pallas_call(
        functools.partial(_ssd_kernel, chunk=chunk),
        grid=(b, s // chunk),
        in_specs=[
            pl.BlockSpec((None, chunk, conv_ch), lambda bi, ci: (bi, ci, xbc_blk)),
            pl.BlockSpec((None, chunk, SSD_WIDTH), lambda bi, ci: (bi, ci, z_blk)),
            pl.BlockSpec((None, chunk, LANES), lambda bi, ci: (bi, ci, 0)),
            const((CONV_WIDTH, conv_ch)), const((1, conv_ch)), const((1, LANES)),
            const((1, SSD_WIDTH)), const((1, SSD_WIDTH)), const((LANES, SSD_WIDTH)),
        ],
        out_specs=pl.BlockSpec((None, chunk, SSD_WIDTH), lambda bi, ci: (bi, ci, 0)),
        out_shape=jax.ShapeDtypeStruct((b, s, SSD_WIDTH), F32),
        scratch_shapes=[pltpu.VMEM((chunk + SUBLANES, conv_ch), F32),
                        pltpu.VMEM((SSD_STATE, SSD_WIDTH), F32)],
        compiler_params=_params("parallel", "arbitrary"),
        name="ssd",
    )(proj3, proj3, dt3, conv_w.astype(F32), conv_b.astype(F32)[None, :], dtb,
      rep(a_log), rep(d_skip), expand)


def _mix_cross_kernel(h_ref, attn_ref, ssd_ref, sbg_ref, ssdg_ref, wout_ref, ng_ref, wq_ref,
                      kv_ref, wo_ref, o_ref, *, scale):
    d = h_ref.shape[1]
    hd = d // X_HEADS
    a = _rms(attn_ref[...], sbg_ref[...]).astype(BF16)
    s = _rms(ssd_ref[...], ssdg_ref[...]).astype(BF16)
    h = h_ref[...] + _dot(a, wout_ref[0:SB_WIDTH, :]) + _dot(s, wout_ref[SB_WIDTH:, :])
    u = _rms(h, ng_ref[...]).astype(BF16)
    q = _dot(u, wq_ref[...]).astype(BF16)
    heads = []
    for hh in range(X_HEADS):
        lanes = slice(hh * hd, (hh + 1) * hd)
        sc = _dot_nt(q[:, lanes], kv_ref[:, lanes]) * scale
        p = jnp.exp(sc - jnp.max(sc, axis=-1, keepdims=True))
        p = p / jnp.sum(p, axis=-1, keepdims=True)
        heads.append(_dot(p.astype(BF16), kv_ref[:, d + hh * hd:d + (hh + 1) * hd]).astype(BF16))
    o_ref[...] = h + _dot(jnp.concatenate(heads, axis=-1), wo_ref[...])


def _mix_cross(h3, attn3, ssd3, sb_g, ssd_g, w_out, norm_g, wq, kv3, wo, *, tm):
    b, s, d = h3.shape
    mem = kv3.shape[1]
    const = lambda shape: pl.BlockSpec(shape, lambda bi, ti: (0,) * len(shape))
    tile = lambda w: pl.BlockSpec((None, tm, w), lambda bi, ti: (bi, ti, 0))
    return pl.pallas_call(
        functools.partial(_mix_cross_kernel, scale=(d // X_HEADS) ** -0.5),
        grid=(b, s // tm),
        in_specs=[tile(d), tile(SB_WIDTH), tile(SSD_WIDTH), const((1, SB_WIDTH)), const((1, SSD_WIDTH)),
                  const((SB_WIDTH + SSD_WIDTH, d)), const((1, d)), const((d, d)),
                  pl.BlockSpec((None, mem, 2 * d), lambda bi, ti: (bi, 0, 0)), const((d, d))],
        out_specs=tile(d),
        out_shape=jax.ShapeDtypeStruct((b, s, d), F32),
        compiler_params=_params("parallel", "parallel"),
        name="mix_cross",
    )(h3, attn3, ssd3, sb_g, ssd_g, w_out, norm_g, wq, kv3, wo)


def _ffn_kernel(h_ref, g_ref, w1_ref, w3_ref, w2_ref, o_ref, u_ref, acc_ref):
    f = pl.program_id(1)

    @pl.when(f == 0)
    def _():
        u_ref[...] = _rms(h_ref[...], g_ref[...]).astype(BF16)
        acc_ref[...] = h_ref[...]

    u = u_ref[...]
    act = (_silu(_dot(u, w1_ref[...])) * _dot(u, w3_ref[...])).astype(BF16)
    acc_ref[...] += _dot(act, w2_ref[...])

    @pl.when(f == pl.num_programs(1) - 1)
    def _():
        o_ref[...] = acc_ref[...]


def _ffn(h, g, w1, w3, w2, *, tm, tf):
    n, d = h.shape
    ff = w1.shape[1]
    return pl.pallas_call(
        _ffn_kernel,
        grid=(n // tm, ff // tf),
        in_specs=[pl.BlockSpec((tm, d), lambda i, f: (i, 0)),
                  pl.BlockSpec((1, d), lambda i, f: (0, 0)),
                  pl.BlockSpec((d, tf), lambda i, f: (0, f)),
                  pl.BlockSpec((d, tf), lambda i, f: (0, f)),
                  pl.BlockSpec((tf, d), lambda i, f: (f, 0))],
        out_specs=pl.BlockSpec((tm, d), lambda i, f: (i, 0)),
        out_shape=jax.ShapeDtypeStruct((n, d), F32),
        scratch_shapes=[pltpu.VMEM((tm, d), BF16), pltpu.VMEM((tm, d), F32)],
        compiler_params=_params("parallel", "arbitrary"),
        name="ffn",
    )(h, g, w1, w3, w2)


def _router_kernel(h_ref, g_ref, wr_ref, u_ref, meta_ref, cnt_ref, run_ref):
    i = pl.program_id(0)
    tm = h_ref.shape[0]

    @pl.when(i == 0)
    def _():
        run_ref[...] = jnp.zeros(run_ref.shape, F32)

    u = _rms(h_ref[...], g_ref[...])
    u_ref[...] = u
    logits = jnp.dot(u, wr_ref[...], preferred_element_type=F32, precision=lax.Precision.HIGHEST)
    lane = lax.broadcasted_iota(jnp.int32, (tm, LANES), 1)
    logits = jnp.where(lane < N_EXPERTS, logits, -jnp.inf)
    m1 = jnp.max(logits, axis=-1, keepdims=True)
    e1 = jnp.min(jnp.where(logits == m1, lane, LANES), axis=-1, keepdims=True)
    rest = jnp.where(lane == e1, -jnp.inf, logits)
    m2 = jnp.max(rest, axis=-1, keepdims=True)
    e2 = jnp.min(jnp.where(rest == m2, lane, LANES), axis=-1, keepdims=True)
    t = jnp.exp(m2 - m1)
    g1 = 1.0 / (1.0 + t)
    g2 = t / (1.0 + t)
    oh1 = (lane == e1).astype(F32)
    oh2 = (lane == e2).astype(F32)
    both = oh1 + oh2
    rows = lax.broadcasted_iota(jnp.int32, (tm, tm), 0)
    cols = lax.broadcasted_iota(jnp.int32, (tm, tm), 1)
    earlier = (cols < rows).astype(BF16)
    before = _dot(earlier, both.astype(BF16)) + run_ref[...]
    r1 = jnp.sum(before * oh1, axis=-1, keepdims=True)
    r2 = jnp.sum(before * oh2, axis=-1, keepdims=True)
    meta = jnp.where(lane == 0, e1.astype(F32), 0.0)
    meta = jnp.where(lane == 1, e2.astype(F32), meta)
    meta = jnp.where(lane == 2, g1, meta)
    meta = jnp.where(lane == 3, g2, meta)
    meta = jnp.where(lane == 4, r1, meta)
    meta = jnp.where(lane == 5, r2, meta)
    meta_ref[...] = meta
    run_ref[...] += jnp.sum(both, axis=0, keepdims=True)
    cnt_ref[...] = run_ref[...]


def _router(h, g, wr, *, tm):
    n, d = h.shape
    return pl.pallas_call(
        _router_kernel,
        grid=(n // tm,),
        in_specs=[pl.BlockSpec((tm, d), lambda i: (i, 0)),
                  pl.BlockSpec((1, d), lambda i: (0, 0)),
                  pl.BlockSpec((d, LANES), lambda i: (0, 0))],
        out_specs=[pl.BlockSpec((tm, d), lambda i: (i, 0)),
                   pl.BlockSpec((tm, LANES), lambda i: (i, 0)),
                   pl.BlockSpec((1, LANES), lambda i: (0, 0))],
        out_shape=[jax.ShapeDtypeStruct((n, d), F32),
                   jax.ShapeDtypeStruct((n, LANES), F32),
                   jax.ShapeDtypeStruct((1, LANES), F32)],
        scratch_shapes=[pltpu.VMEM((1, LANES), F32)],
        compiler_params=_params("arbitrary"),
        name="router",
    )(h, g, wr)


def _scatter_kernel(dest_ref, u_ref, init_ref, xs_ref, sem):
    del init_ref
    tm = u_ref.shape[0]

    def row_copy(r, k):
        return pltpu.make_async_copy(u_ref.at[pl.ds(r, 1)],
                                     xs_ref.at[pl.ds(dest_ref[0, 2 * r + k], 1)], sem)

    def start(r, _):
        row_copy(r, 0).start()
        row_copy(r, 1).start()
        return 0

    def wait(r, _):
        row_copy(r, 0).wait()
        row_copy(r, 1).wait()
        return 0

    lax.fori_loop(0, tm, start, 0)
    lax.fori_loop(0, tm, wait, 0)


def _dest_spec(tm):
    return pl.BlockSpec((None, 1, 2 * tm), lambda i: (i, 0, 0), memory_space=pltpu.SMEM)


def _scatter_rows(dest, u, cap, *, tm):
    n, d = u.shape
    init = jnp.zeros((cap, d), u.dtype)
    return pl.pallas_call(
        _scatter_kernel,
        grid=(n // tm,),
        in_specs=[_dest_spec(tm),
                  pl.BlockSpec((tm, d), lambda i: (i, 0)),
                  pl.BlockSpec(memory_space=pl.ANY)],
        out_specs=pl.BlockSpec(memory_space=pl.ANY),
        out_shape=jax.ShapeDtypeStruct((cap, d), u.dtype),
        scratch_shapes=[pltpu.SemaphoreType.DMA],
        input_output_aliases={2: 0},
        compiler_params=_params("arbitrary"),
        name="moe_scatter",
    )(dest, u, init)


def _expert_kernel(be_ref, used_ref, x_ref, w1_ref, w3_ref, w2_ref, o_ref, acc_ref):
    del be_ref
    i = pl.program_id(0)
    f = pl.program_id(1)

    @pl.when(i < used_ref[0])
    def _():
        x = x_ref[...].astype(BF16)
        act = (_silu(_dot(x, w1_ref[...])) * _dot(x, w3_ref[...])).astype(BF16)
        part = _dot(act, w2_ref[...])

        @pl.when(f == 0)
        def _():
            acc_ref[...] = part

        @pl.when(f > 0)
        def _():
            acc_ref[...] += part

        @pl.when(f == pl.num_programs(1) - 1)
        def _():
            o_ref[...] = acc_ref[...]

    @pl.when(jnp.logical_and(i >= used_ref[0], f == pl.num_programs(1) - 1))
    def _():
        o_ref[...] = jnp.zeros(o_ref.shape, F32)


def _experts(block_e, used, xs, w1, w3, w2, *, tm, tf):
    cap, d = xs.shape
    ff = w1.shape[2]
    return pl.pallas_call(
        _expert_kernel,
        grid_spec=pltpu.PrefetchScalarGridSpec(
            num_scalar_prefetch=2,
            grid=(cap // tm, ff // tf),
            in_specs=[pl.BlockSpec((tm, d), lambda i, f, be, nu: (i, 0)),
                      pl.BlockSpec((None, d, tf), lambda i, f, be, nu: (be[i], 0, f)),
                      pl.BlockSpec((None, d, tf), lambda i, f, be, nu: (be[i], 0, f)),
                      pl.BlockSpec((None, tf, d), lambda i, f, be, nu: (be[i], f, 0))],
            out_specs=pl.BlockSpec((tm, d), lambda i, f, be, nu: (i, 0)),
            scratch_shapes=[pltpu.VMEM((tm, d), F32)],
        ),
        out_shape=jax.ShapeDtypeStruct((cap, d), F32),
        compiler_params=_params("parallel", "arbitrary"),
        name="moe_experts",
    )(block_e, used, xs, w1, w3, w2)


def _combine_kernel(dest_ref, h_ref, meta_ref, y_ref, fg_ref, o_ref, buf_ref, sem, *, final_norm):
    tm = h_ref.shape[0]

    def row_copy(r, k):
        return pltpu.make_async_copy(y_ref.at[pl.ds(dest_ref[0, 2 * r + k], 1)],
                                     buf_ref.at[k, pl.ds(r, 1)], sem)

    def start(r, _):
        row_copy(r, 0).start()
        row_copy(r, 1).start()
        return 0

    def wait(r, _):
        row_copy(r, 0).wait()
        row_copy(r, 1).wait()
        return 0

    lax.fori_loop(0, tm, start, 0)
    lax.fori_loop(0, tm, wait, 0)
    meta = meta_ref[...]
    out = h_ref[...] + meta[:, 2:3] * buf_ref[0] + meta[:, 3:4] * buf_ref[1]
    if final_norm:
        out = _rms(out, fg_ref[...])
    o_ref[...] = out


def _combine(dest, h, meta, y, fg, *, tm, final_norm):
    n, d = h.shape
    return pl.pallas_call(
        functools.partial(_combine_kernel, final_norm=final_norm),
        grid=(n // tm,),
        in_specs=[_dest_spec(tm),
                  pl.BlockSpec((tm, d), lambda i: (i, 0)),
                  pl.BlockSpec((tm, LANES), lambda i: (i, 0)),
                  pl.BlockSpec(memory_space=pl.ANY),
                  pl.BlockSpec((1, d), lambda i: (0, 0))],
        out_specs=pl.BlockSpec((tm, d), lambda i: (i, 0)),
        out_shape=jax.ShapeDtypeStruct((n, d), F32),
        scratch_shapes=[pltpu.VMEM((2, tm, d), F32), pltpu.SemaphoreType.DMA],
        compiler_params=_params("arbitrary"),
        name="moe_combine",
    )(dest, h, meta, y, fg)


def _final_norm_kernel(h_ref, g_ref, o_ref):
    o_ref[...] = _rms(h_ref[...], g_ref[...])


def _final_norm(h, g, *, tm):
    n, d = h.shape
    return pl.pallas_call(
        _final_norm_kernel,
        grid=(n // tm,),
        in_specs=[pl.BlockSpec((tm, d), lambda i: (i, 0)), pl.BlockSpec((1, d), lambda i: (0, 0))],
        out_specs=pl.BlockSpec((tm, d), lambda i: (i, 0)),
        out_shape=jax.ShapeDtypeStruct((n, d), F32),
        compiler_params=_params("parallel"),
        name="final_norm",
    )(h, g)


def _moe(h, g, router_w, w1, w3, w2, final_g, *, final_norm):
    n, d = h.shape
    tm = MOE_TILE
    wr = jnp.zeros((d, LANES), F32).at[:, :N_EXPERTS].set(router_w.astype(F32))
    u, meta, counts = _router(h, g, wr, tm=ROW_TILE)
    counts = counts[0, :N_EXPERTS].astype(jnp.int32)
    padded = (counts + tm - 1) // tm * tm
    pends = jnp.cumsum(padded)
    pstarts = pends - padded
    experts = meta[:, 0:2].astype(jnp.int32)
    dest = (pstarts[experts] + meta[:, 4:6].astype(jnp.int32)).reshape(n // ROW_TILE, 1, 2 * ROW_TILE)
    cap = (-(-2 * n // tm)) * tm + N_EXPERTS * tm
    block_start = jnp.arange(cap // tm, dtype=jnp.int32) * tm
    block_e = jnp.minimum(jnp.sum((block_start[:, None] >= pends[None, :]).astype(jnp.int32), axis=1),
                          N_EXPERTS - 1)
    used = (pends[-1:] // tm).astype(jnp.int32)
    xs = _scatter_rows(dest, u, cap, tm=ROW_TILE)
    y = _experts(block_e, used, xs, w1, w3, w2, tm=tm, tf=896)
    return _combine(dest, h, meta, y, final_g, tm=ROW_TILE, final_norm=final_norm)


def kernel(x, mem, mem_norm, norm_mix, w_in, conv_w, conv_b, dt_bias, a_log, d_skip, sb_norm, ssd_norm, w_out, norm_cross, wq_x, wk_x, wv_x, wo_x, norm_ffn, ffn_w1, ffn_w3, ffn_w2, router, moe_w1, moe_w3, moe_w2, final_norm):
    b, s, d = x.shape
    depth = w_in.shape[0]
    n = b * s
    mlen = mem.shape[1]
    main_cols = 3 * SB_WIDTH + SSD_WIDTH + conv_w.shape[2]
    row = lambda v: v.astype(F32)[None, :]
    h = x.astype(F32).reshape(n, d)
    mem2 = mem.astype(F32).reshape(b * mlen, d)
    for i in range(depth):
        w_main = w_in[i, :, :main_cols].astype(BF16)
        w_dt = jnp.zeros((d, LANES), BF16).at[:, :SSD_HEADS].set(w_in[i, :, main_cols:].astype(BF16))
        proj, dt_raw = _norm_proj(h, row(norm_mix[i]), w_main, w_dt, tm=ROW_TILE)
        proj3 = proj.reshape(b, s, main_cols)
        attn = _sb_attention(proj3, blk=SB_BLOCK)
        ssd = _ssd(proj3, dt_raw.reshape(b, s, LANES), conv_w[i], conv_b[i], dt_bias[i], a_log[i],
                   d_skip[i], chunk=SSD_CHUNK)
        w_kv = jnp.concatenate([wk_x[i], wv_x[i]], axis=1).astype(BF16)
        kv = _norm_proj(mem2, row(mem_norm), w_kv, tm=ROW_TILE)
        h = _mix_cross(h.reshape(b, s, d), attn, ssd, row(sb_norm[i]), row(ssd_norm[i]),
                       w_out[i].astype(BF16), row(norm_cross[i]), wq_x[i].astype(BF16),
                       kv.reshape(b, mlen, 2 * d), wo_x[i].astype(BF16), tm=ROW_TILE).reshape(n, d)
        j = i // 2
        last = i == depth - 1
        if i % 2 == 0:
            h = _ffn(h, row(norm_ffn[i]), ffn_w1[j].astype(BF16), ffn_w3[j].astype(BF16),
                     ffn_w2[j].astype(BF16), tm=ROW_TILE, tf=1408)
            if last:
                h = _final_norm(h, row(final_norm), tm=ROW_TILE)
        else:
            h = _moe(h, row(norm_ffn[i]), router[j], moe_w1[j].astype(BF16), moe_w3[j].astype(BF16),
                     moe_w2[j].astype(BF16), row(final_norm), final_norm=last)
    return h.reshape(b, s, d)
```

```python
import functools

import jax
import jax.numpy as jnp
from jax import lax
from jax.experimental import pallas as pl
from jax.experimental.pallas import tpu as pltpu

F32 = jnp.float32
BF16 = jnp.bfloat16

EPS = 1e-6
HEAD_DIM = 64
SB_WIDTH = 512
SSD_WIDTH = 512
SSD_HEADS = 8
SSD_STATE = 128
SSD_GROUPS = 2
CONV_WIDTH = 4
X_HEADS = 4
N_EXPERTS = 8
LANES = 128
SUBLANES = 8
VMEM_LIMIT_BYTES = 56 * 1024 * 1024
UNDERFLOW_LOG = 105.0

SB_BLOCK = 256
SB_HEADS_PER_STEP = 8
SSD_CHUNK = 256
ROW_TILE = 512
MOE_TILE = 512


def _params(*sem):
    return pltpu.CompilerParams(dimension_semantics=sem, vmem_limit_bytes=VMEM_LIMIT_BYTES)


def _rms(x, g):
    return x * lax.rsqrt(jnp.mean(x * x, axis=-1, keepdims=True) + EPS) * g


def _silu(x):
    return x / (1.0 + jnp.exp(-x))


def _dot(a, b):
    return jnp.dot(a, b, preferred_element_type=F32)


def _dot_nt(a, b):
    return lax.dot_general(a, b, (((1,), (1,)), ((), ())), preferred_element_type=F32)


def _dot_tn(a, b):
    return lax.dot_general(a, b, (((0,), (0,)), ((), ())), preferred_element_type=F32)


def _split2(x):
    hi = x.astype(BF16)
    lo = (x - hi.astype(F32)).astype(BF16)
    return hi, lo


def _split3(x):
    hi = x.astype(BF16)
    r = x - hi.astype(F32)
    mid = r.astype(BF16)
    lo = (r - mid.astype(F32)).astype(BF16)
    return hi, mid, lo


def _norm_proj_kernel(x_ref, g_ref, w_ref, *rest, col_chunk, with_aux):
    if with_aux:
        waux_ref, o_ref, oaux_ref = rest
    else:
        (o_ref,) = rest
    u = _rms(x_ref[...], g_ref[...]).astype(BF16)
    for c in range(o_ref.shape[1] // col_chunk):
        sl = slice(c * col_chunk, (c + 1) * col_chunk)
        o_ref[:, sl] = _dot(u, w_ref[:, sl]).astype(o_ref.dtype)
    if with_aux:
        oaux_ref[...] = _dot(u, waux_ref[...])


def _norm_proj(x, g, w, w_aux=None, *, tm):
    n, d = x.shape
    m = w.shape[1]
    with_aux = w_aux is not None
    in_specs = [
        pl.BlockSpec((tm, d), lambda i: (i, 0)),
        pl.BlockSpec((1, d), lambda i: (0, 0)),
        pl.BlockSpec((d, m), lambda i: (0, 0)),
    ]
    out_shape = [jax.ShapeDtypeStruct((n, m), BF16)]
    out_specs = [pl.BlockSpec((tm, m), lambda i: (i, 0))]
    args = [x, g, w]
    if with_aux:
        in_specs.append(pl.BlockSpec((d, LANES), lambda i: (0, 0)))
        out_shape.append(jax.ShapeDtypeStruct((n, LANES), F32))
        out_specs.append(pl.BlockSpec((tm, LANES), lambda i: (i, 0)))
        args.append(w_aux)
    res = pl.pallas_call(
        functools.partial(_norm_proj_kernel, col_chunk=512, with_aux=with_aux),
        grid=(n // tm,),
        in_specs=in_specs,
        out_specs=out_specs,
        out_shape=out_shape,
        compiler_params=_params("parallel"),
        name="norm_proj_aux" if with_aux else "norm_proj",
    )(*args)
    return res if with_aux else res[0]


def _sb_kernel(q_ref, k_ref, v_ref, o_ref, carry_ref, acc_ref, *, blk, scale):
    nblk = q_ref.shape[0] // blk
    nheads = q_ref.shape[1] // HEAD_DIM
    rows = lax.broadcasted_iota(jnp.int32, (blk, blk), 0)
    cols = lax.broadcasted_iota(jnp.int32, (blk, blk), 1)
    suffix_ones = (rows >= cols).astype(BF16)
    before = cols < rows
    heads = [slice(hh * HEAD_DIM, (hh + 1) * HEAD_DIM) for hh in range(nheads)]

    def all_heads(qrows, kb, diag):
        ks = pl.ds(pl.multiple_of(kb * blk, blk), blk)
        zs = [_dot_nt(q_ref[qrows, lanes] * scale, k_ref[ks, lanes]) for lanes in heads]
        fails = [jnp.maximum(z, 0.0) + jnp.log(1.0 + jnp.exp(-jnp.abs(z))) for z in zs]
        if diag:
            fails = [jnp.where(before, f, 0.0) for f in fails]
        sufs = [_dot(f.astype(BF16), suffix_ones) for f in fails]
        ws = [jnp.exp(z - suf) for z, suf in zip(zs, sufs)]
        if diag:
            ws = [jnp.where(before, w, 0.0) for w in ws]
        pvs = [_dot(w.astype(BF16), v_ref[ks, lanes]) for w, lanes in zip(ws, heads)]
        low = None
        for hh in range(nheads):
            if diag:
                carry = sufs[hh][:, 0:1]
                acc_ref[hh] = pvs[hh]
            else:
                carry = carry_ref[hh]
                acc_ref[hh] += jnp.exp(-carry) * pvs[hh]
                carry = carry + sufs[hh][:, 0:1]
            carry_ref[hh] = carry
            low = jnp.min(carry) if low is None else jnp.minimum(low, jnp.min(carry))
        return low

    def q_block(i, _):
        qrows = pl.ds(pl.multiple_of(i * blk, blk), blk)
        low = all_heads(qrows, i, True)

        def live(st):
            return jnp.logical_and(st[0] >= 0, st[1] < UNDERFLOW_LOG)

        def body(st):
            return st[0] - 1, all_heads(qrows, st[0], False)

        lax.while_loop(live, body, (i - 1, low))
        o_ref[qrows, :] = jnp.concatenate([acc_ref[hh] for hh in range(nheads)], axis=-1)
        return 0

    lax.fori_loop(0, nblk, q_block, 0)


def _sb_attention(proj3, *, blk, heads_per_step):
    b, s, _ = proj3.shape
    width = heads_per_step * HEAD_DIM
    groups = SB_WIDTH // width
    spec = lambda off: pl.BlockSpec((None, s, width), lambda bi, hg: (bi, 0, off + hg))
    return pl.pallas_call(
        functools.partial(_sb_kernel, blk=blk, scale=HEAD_DIM ** -0.5),
        grid=(b, groups),
        in_specs=[spec(0), spec(groups), spec(2 * groups)],
        out_specs=pl.BlockSpec((None, s, width), lambda bi, hg: (bi, 0, hg)),
        out_shape=jax.ShapeDtypeStruct((b, s, SB_WIDTH), F32),
        scratch_shapes=[pltpu.VMEM((heads_per_step, blk, 1), F32),
                        pltpu.VMEM((heads_per_step, blk, HEAD_DIM), F32)],
        compiler_params=_params("parallel", "parallel"),
        name="stick_breaking",
    )(proj3, proj3, proj3)


def _ssd_kernel(xbc_ref, z_ref, dt_ref, cw_ref, cb_ref, dtb_ref, alog_ref, dskip_ref, expand_ref,
                o_ref, xpad_ref, state_ref, *, chunk):
    c = pl.program_id(1)
    gn = SSD_GROUPS * SSD_STATE
    half = SSD_WIDTH // SSD_GROUPS

    @pl.when(c == 0)
    def _():
        xpad_ref[0:SUBLANES, :] = jnp.zeros((SUBLANES, xpad_ref.shape[1]), F32)
        state_ref[...] = jnp.zeros(state_ref.shape, F32)

    @pl.when(c > 0)
    def _():
        xpad_ref[0:SUBLANES, :] = xpad_ref[chunk:chunk + SUBLANES, :]

    xpad_ref[SUBLANES:SUBLANES + chunk, :] = xbc_ref[...].astype(F32)
    conv = cb_ref[...]
    for tap in range(CONV_WIDTH):
        start = SUBLANES - (CONV_WIDTH - 1) + tap
        conv = conv + cw_ref[tap:tap + 1, :] * xpad_ref[start:start + chunk, :]
    act = _silu(conv)
    xs = act[:, :SSD_WIDTH]
    bm = act[:, SSD_WIDTH:SSD_WIDTH + gn].astype(BF16)
    cm = act[:, SSD_WIDTH + gn:].astype(BF16)

    raw = dt_ref[...] + dtb_ref[...]
    dt8 = jnp.maximum(raw, 0.0) + jnp.log(1.0 + jnp.exp(-jnp.abs(raw)))
    expand = expand_ref[...]
    d1, d2, d3 = _split3(dt8)
    dt = _dot(d1, expand) + _dot(d2, expand) + _dot(d3, expand)
    a = -jnp.exp(alog_ref[...])
    rows = lax.broadcasted_iota(jnp.int32, (chunk, chunk), 0)
    cols = lax.broadcasted_iota(jnp.int32, (chunk, chunk), 1)
    causal = rows >= cols
    prefix_ones = causal.astype(BF16)
    l1, l2, l3 = _split3(dt * a)
    la = _dot(prefix_ones, l1) + _dot(prefix_ones, l2) + _dot(prefix_ones, l3)
    la_t = jnp.transpose(la)
    xdt = xs * dt
    xdt_b = xdt.astype(BF16)

    y_parts = []
    for g in range(SSD_GROUPS):
        cb = _dot_nt(cm[:, g * SSD_STATE:(g + 1) * SSD_STATE], bm[:, g * SSD_STATE:(g + 1) * SSD_STATE])
        for r in range(SSD_HEADS // SSD_GROUPS):
            lane0 = (g * (SSD_HEADS // SSD_GROUPS) + r) * HEAD_DIM
            diff = la[:, lane0:lane0 + 1] - la_t[lane0:lane0 + 1, :]
            m = cb * jnp.exp(jnp.where(causal, diff, -jnp.inf))
            y_parts.append(_dot(m.astype(BF16), xdt_b[:, lane0:lane0 + HEAD_DIM]))
    y = jnp.concatenate(y_parts, axis=-1)

    state = state_ref[...]
    state_b = state.astype(BF16)
    y_off = jnp.concatenate(
        [_dot(cm[:, g * SSD_STATE:(g + 1) * SSD_STATE], state_b[:, g * half:(g + 1) * half])
         for g in range(SSD_GROUPS)], axis=-1)
    y = y + y_off * jnp.exp(la)
    la_last = la[chunk - 1:chunk, :]
    xdec = (xdt * jnp.exp(la_last - la)).astype(BF16)
    upd = jnp.concatenate(
        [_dot_tn(bm[:, g * SSD_STATE:(g + 1) * SSD_STATE], xdec[:, g * half:(g + 1) * half])
         for g in range(SSD_GROUPS)], axis=-1)
    state_ref[...] = state * jnp.exp(la_last) + upd

    y = y + xs * dskip_ref[...]
    o_ref[...] = y * _silu(z_ref[...].astype(F32))


def _ssd(proj3, dt3, conv_w, conv_b, dt_bias, a_log, d_skip, *, chunk):
    b, s, _ = proj3.shape
    conv_ch = conv_w.shape[1]
    rep = lambda v: jnp.repeat(v.astype(F32), HEAD_DIM)[None, :]
    dtb = jnp.zeros((1, LANES), F32).at[0, :SSD_HEADS].set(dt_bias.astype(F32))
    expand = (jnp.arange(LANES)[:, None] == (jnp.arange(SSD_WIDTH) // HEAD_DIM)[None, :]).astype(BF16)
    z_blk = (3 * SB_WIDTH) // SSD_WIDTH
    xbc_blk = (3 * SB_WIDTH + SSD_WIDTH) // conv_ch
    const = lambda shape: pl.BlockSpec(shape, lambda bi, ci: (0,) * len(shape))
    return pl.pallas_call(
        functools.partial(_ssd_kernel, chunk=chunk),
        grid=(b, s // chunk),
        in_specs=[
            pl.BlockSpec((None, chunk, conv_ch), lambda bi, ci: (bi, ci, xbc_blk)),
            pl.BlockSpec((None, chunk, SSD_WIDTH), lambda bi, ci: (bi, ci, z_blk)),
            pl.BlockSpec((None, chunk, LANES), lambda bi, ci: (bi, ci, 0)),
            const((CONV_WIDTH, conv_ch)), const((1, conv_ch)), const((1, LANES)),
            const((1, SSD_WIDTH)), const((1, SSD_WIDTH)), const((LANES, SSD_WIDTH)),
        ],
        out_specs=pl.BlockSpec((None, chunk, SSD_WIDTH), lambda bi, ci: (bi, ci, 0)),
        out_shape=jax.ShapeDtypeStruct((b, s, SSD_WIDTH), F32),
        scratch_shapes=[pltpu.VMEM((chunk + SUBLANES, conv_ch), F32),
                        pltpu.VMEM((SSD_STATE, SSD_WIDTH), F32)],
        compiler_params=_params("parallel", "arbitrary"),
        name="ssd",
    )(proj3, proj3, dt3, conv_w.astype(F32), conv_b.astype(F32)[None, :], dtb,
      rep(a_log), rep(d_skip), expand)


def _mix_cross_kernel(h_ref, attn_ref, ssd_ref, sbg_ref, ssdg_ref, wout_ref, ng_ref, wq_ref,
                      kv_ref, wo_ref, o_ref, *, scale):
    d = h_ref.shape[1]
    hd = d // X_HEADS
    a = _rms(attn_ref[...], sbg_ref[...]).astype(BF16)
    s = _rms(ssd_ref[...], ssdg_ref[...]).astype(BF16)
    h = h_ref[...] + _dot(a, wout_ref[0:SB_WIDTH, :]) + _dot(s, wout_ref[SB_WIDTH:, :])
    u = _rms(h, ng_ref[...]).astype(BF16)
    q = _dot(u, wq_ref[...]).astype(BF16)
    heads = []
    for hh in range(X_HEADS):
        lanes = slice(hh * hd, (hh + 1) * hd)
        sc = _dot_nt(q[:, lanes], kv_ref[:, lanes]) * scale
        p = jnp.exp(sc - jnp.max(sc, axis=-1, keepdims=True))
        p = p / jnp.sum(p, axis=-1, keepdims=True)
        heads.append(_dot(p.astype(BF16), kv_ref[:, d + hh * hd:d + (hh + 1) * hd]).astype(BF16))
    o_ref[...] = h + _dot(jnp.concatenate(heads, axis=-1), wo_ref[...])


def _mix_cross(h3, attn3, ssd3, sb_g, ssd_g, w_out, norm_g, wq, kv3, wo, *, tm):
    b, s, d = h3.shape
    mem = kv3.shape[1]
    const = lambda shape: pl.BlockSpec(shape, lambda bi, ti: (0,) * len(shape))
    tile = lambda w: pl.BlockSpec((None, tm, w), lambda bi, ti: (bi, ti, 0))
    return pl.pallas_call(
        functools.partial(_mix_cross_kernel, scale=(d // X_HEADS) ** -0.5),
        grid=(b, s // tm),
        in_specs=[tile(d), tile(SB_WIDTH), tile(SSD_WIDTH), const((1, SB_WIDTH)), const((1, SSD_WIDTH)),
                  const((SB_WIDTH + SSD_WIDTH, d)), const((1, d)), const((d, d)),
                  pl.BlockSpec((None, mem, 2 * d), lambda bi, ti: (bi, 0, 0)), const((d, d))],
        out_specs=tile(d),
        out_shape=jax.ShapeDtypeStruct((b, s, d), F32),
        compiler_params=_params("parallel", "parallel"),
        name="mix_cross",
    )(h3, attn3, ssd3, sb_g, ssd_g, w_out, norm_g, wq, kv3, wo)


def _ffn_kernel(h_ref, g_ref, w1_ref, w3_ref, w2_ref, o_ref, u_ref, acc_ref):
    f = pl.program_id(1)

    @pl.when(f == 0)
    def _():
        u_ref[...] = _rms(h_ref[...], g_ref[...]).astype(BF16)
        acc_ref[...] = h_ref[...]

    u = u_ref[...]
    act = (_silu(_dot(u, w1_ref[...])) * _dot(u, w3_ref[...])).astype(BF16)
    acc_ref[...] += _dot(act, w2_ref[...])

    @pl.when(f == pl.num_programs(1) - 1)
    def _():
        o_ref[...] = acc_ref[...]


def _ffn(h, g, w1, w3, w2, *, tm, tf):
    n, d = h.shape
    ff = w1.shape[1]
    return pl.pallas_call(
        _ffn_kernel,
        grid=(n // tm, ff // tf),
        in_specs=[pl.BlockSpec((tm, d), lambda i, f: (i, 0)),
                  pl.BlockSpec((1, d), lambda i, f: (0, 0)),
                  pl.BlockSpec((d, tf), lambda i, f: (0, f)),
                  pl.BlockSpec((d, tf), lambda i, f: (0, f)),
                  pl.BlockSpec((tf, d), lambda i, f: (f, 0))],
        out_specs=pl.BlockSpec((tm, d), lambda i, f: (i, 0)),
        out_shape=jax.ShapeDtypeStruct((n, d), F32),
        scratch_shapes=[pltpu.VMEM((tm, d), BF16), pltpu.VMEM((tm, d), F32)],
        compiler_params=_params("parallel", "arbitrary"),
        name="ffn",
    )(h, g, w1, w3, w2)


def _router_kernel(h_ref, g_ref, wr_ref, u_ref, meta_ref, cnt_ref, run_ref):
    i = pl.program_id(0)
    tm = h_ref.shape[0]

    @pl.when(i == 0)
    def _():
        run_ref[...] = jnp.zeros(run_ref.shape, F32)

    u = _rms(h_ref[...], g_ref[...])
    u_ref[...] = u
    logits = jnp.dot(u, wr_ref[...], preferred_element_type=F32, precision=lax.Precision.HIGHEST)
    lane = lax.broadcasted_iota(jnp.int32, (tm, LANES), 1)
    logits = jnp.where(lane < N_EXPERTS, logits, -jnp.inf)
    m1 = jnp.max(logits, axis=-1, keepdims=True)
    e1 = jnp.min(jnp.where(logits == m1, lane, LANES), axis=-1, keepdims=True)
    rest = jnp.where(lane == e1, -jnp.inf, logits)
    m2 = jnp.max(rest, axis=-1, keepdims=True)
    e2 = jnp.min(jnp.where(rest == m2, lane, LANES), axis=-1, keepdims=True)
    t = jnp.exp(m2 - m1)
    g1 = 1.0 / (1.0 + t)
    g2 = t / (1.0 + t)
    oh1 = (lane == e1).astype(F32)
    oh2 = (lane == e2).astype(F32)
    both = oh1 + oh2
    rows = lax.broadcasted_iota(jnp.int32, (tm, tm), 0)
    cols = lax.broadcasted_iota(jnp.int32, (tm, tm), 1)
    earlier = (cols < rows).astype(BF16)
    before = _dot(earlier, both.astype(BF16)) + run_ref[...]
    r1 = jnp.sum(before * oh1, axis=-1, keepdims=True)
    r2 = jnp.sum(before * oh2, axis=-1, keepdims=True)
    meta = jnp.where(lane == 0, e1.astype(F32), 0.0)
    meta = jnp.where(lane == 1, e2.astype(F32), meta)
    meta = jnp.where(lane == 2, g1, meta)
    meta = jnp.where(lane == 3, g2, meta)
    meta = jnp.where(lane == 4, r1, meta)
    meta = jnp.where(lane == 5, r2, meta)
    meta_ref[...] = meta
    run_ref[...] += jnp.sum(both, axis=0, keepdims=True)
    cnt_ref[...] = run_ref[...]


def _router(h, g, wr, *, tm):
    n, d = h.shape
    return pl.pallas_call(
        _router_kernel,
        grid=(n // tm,),
        in_specs=[pl.BlockSpec((tm, d), lambda i: (i, 0)),
                  pl.BlockSpec((1, d), lambda i: (0, 0)),
                  pl.BlockSpec((d, LANES), lambda i: (0, 0))],
        out_specs=[pl.BlockSpec((tm, d), lambda i: (i, 0)),
                   pl.BlockSpec((tm, LANES), lambda i: (i, 0)),
                   pl.BlockSpec((1, LANES), lambda i: (0, 0))],
        out_shape=[jax.ShapeDtypeStruct((n, d), F32),
                   jax.ShapeDtypeStruct((n, LANES), F32),
                   jax.ShapeDtypeStruct((1, LANES), F32)],
        scratch_shapes=[pltpu.VMEM((1, LANES), F32)],
        compiler_params=_params("arbitrary"),
        name="router",
    )(h, g, wr)


def _scatter_kernel(dest_ref, u_ref, init_ref, xs_ref, sem):
    del init_ref
    tm = u_ref.shape[0]

    def row_copy(r, k):
        return pltpu.make_async_copy(u_ref.at[pl.ds(r, 1)],
                                     xs_ref.at[pl.ds(dest_ref[0, 2 * r + k], 1)], sem)

    def start(r, _):
        row_copy(r, 0).start()
        row_copy(r, 1).start()
        return 0

    def wait(r, _):
        row_copy(r, 0).wait()
        row_copy(r, 1).wait()
        return 0

    lax.fori_loop(0, tm, start, 0)
    lax.fori_loop(0, tm, wait, 0)


def _dest_spec(tm):
    return pl.BlockSpec((None, 1, 2 * tm), lambda i: (i, 0, 0), memory_space=pltpu.SMEM)


def _scatter_rows(dest, u, cap, *, tm):
    n, d = u.shape
    init = jnp.zeros((cap, d), u.dtype)
    return pl.pallas_call(
        _scatter_kernel,
        grid=(n // tm,),
        in_specs=[_dest_spec(tm),
                  pl.BlockSpec((tm, d), lambda i: (i, 0)),
                  pl.BlockSpec(memory_space=pl.ANY)],
        out_specs=pl.BlockSpec(memory_space=pl.ANY),
        out_shape=jax.ShapeDtypeStruct((cap, d), u.dtype),
        scratch_shapes=[pltpu.SemaphoreType.DMA],
        input_output_aliases={2: 0},
        compiler_params=_params("arbitrary"),
        name="moe_scatter",
    )(dest, u, init)


def _expert_kernel(be_ref, used_ref, x_ref, w1_ref, w3_ref, w2_ref, o_ref, acc_ref):
    del be_ref
    i = pl.program_id(0)
    f = pl.program_id(1)

    @pl.when(i < used_ref[0])
    def _():
        x = x_ref[...].astype(BF16)
        act = (_silu(_dot(x, w1_ref[...])) * _dot(x, w3_ref[...])).astype(BF16)
        part = _dot(act, w2_ref[...])

        @pl.when(f == 0)
        def _():
            acc_ref[...] = part

        @pl.when(f > 0)
        def _():
            acc_ref[...] += part

        @pl.when(f == pl.num_programs(1) - 1)
        def _():
            o_ref[...] = acc_ref[...]

    @pl.when(jnp.logical_and(i >= used_ref[0], f == pl.num_programs(1) - 1))
    def _():
        o_ref[...] = jnp.zeros(o_ref.shape, F32)


def _experts(block_e, used, xs, w1, w3, w2, *, tm, tf):
    cap, d = xs.shape
    ff = w1.shape[2]
    return pl.pallas_call(
        _expert_kernel,
        grid_spec=pltpu.PrefetchScalarGridSpec(
            num_scalar_prefetch=2,
            grid=(cap // tm, ff // tf),
            in_specs=[pl.BlockSpec((tm, d), lambda i, f, be, nu: (i, 0)),
                      pl.BlockSpec((None, d, tf), lambda i, f, be, nu: (be[i], 0, f)),
                      pl.BlockSpec((None, d, tf), lambda i, f, be, nu: (be[i], 0, f)),
                      pl.BlockSpec((None, tf, d), lambda i, f, be, nu: (be[i], f, 0))],
            out_specs=pl.BlockSpec((tm, d), lambda i, f, be, nu: (i, 0)),
            scratch_shapes=[pltpu.VMEM((tm, d), F32)],
        ),
        out_shape=jax.ShapeDtypeStruct((cap, d), F32),
        compiler_params=_params("parallel", "arbitrary"),
        name="moe_experts",
    )(block_e, used, xs, w1, w3, w2)


def _combine_kernel(dest_ref, h_ref, meta_ref, y_ref, fg_ref, o_ref, buf_ref, sem, *, final_norm):
    tm = h_ref.shape[0]

    def row_copy(r, k):
        return pltpu.make_async_copy(y_ref.at[pl.ds(dest_ref[0, 2 * r + k], 1)],
                                     buf_ref.at[k, pl.ds(r, 1)], sem)

    def start(r, _):
        row_copy(r, 0).start()
        row_copy(r, 1).start()
        return 0

    def wait(r, _):
        row_copy(r, 0).wait()
        row_copy(r, 1).wait()
        return 0

    lax.fori_loop(0, tm, start, 0)
    lax.fori_loop(0, tm, wait, 0)
    meta = meta_ref[...]
    out = h_ref[...] + meta[:, 2:3] * buf_ref[0] + meta[:, 3:4] * buf_ref[1]
    if final_norm:
        out = _rms(out, fg_ref[...])
    o_ref[...] = out


def _combine(dest, h, meta, y, fg, *, tm, final_norm):
    n, d = h.shape
    return pl.pallas_call(
        functools.partial(_combine_kernel, final_norm=final_norm),
        grid=(n // tm,),
        in_specs=[_dest_spec(tm),
                  pl.BlockSpec((tm, d), lambda i: (i, 0)),
                  pl.BlockSpec((tm, LANES), lambda i: (i, 0)),
                  pl.BlockSpec(memory_space=pl.ANY),
                  pl.BlockSpec((1, d), lambda i: (0, 0))],
        out_specs=pl.BlockSpec((tm, d), lambda i: (i, 0)),
        out_shape=jax.ShapeDtypeStruct((n, d), F32),
        scratch_shapes=[pltpu.VMEM((2, tm, d), F32), pltpu.SemaphoreType.DMA],
        compiler_params=_params("arbitrary"),
        name="moe_combine",
    )(dest, h, meta, y, fg)


def _final_norm_kernel(h_ref, g_ref, o_ref):
    o_ref[...] = _rms(h_ref[...], g_ref[...])


def _final_norm(h, g, *, tm):
    n, d = h.shape
    return pl.pallas_call(
        _final_norm_kernel,
        grid=(n // tm,),
        in_specs=[pl.BlockSpec((tm, d), lambda i: (i, 0)), pl.BlockSpec((1, d), lambda i: (0, 0))],
        out_specs=pl.BlockSpec((tm, d), lambda i: (i, 0)),
        out_shape=jax.ShapeDtypeStruct((n, d), F32),
        compiler_params=_params("parallel"),
        name="final_norm",
    )(h, g)


def _moe(h, g, router_w, w1, w3, w2, final_g, *, final_norm):
    n, d = h.shape
    tm = MOE_TILE
    wr = jnp.zeros((d, LANES), F32).at[:, :N_EXPERTS].set(router_w.astype(F32))
    u, meta, counts = _router(h, g, wr, tm=ROW_TILE)
    counts = counts[0, :N_EXPERTS].astype(jnp.int32)
    padded = (counts + tm - 1) // tm * tm
    pends = jnp.cumsum(padded)
    pstarts = pends - padded
    experts = meta[:, 0:2].astype(jnp.int32)
    dest = (pstarts[experts] + meta[:, 4:6].astype(jnp.int32)).reshape(n // ROW_TILE, 1, 2 * ROW_TILE)
    cap = (-(-2 * n // tm)) * tm + N_EXPERTS * tm
    block_start = jnp.arange(cap // tm, dtype=jnp.int32) * tm
    block_e = jnp.minimum(jnp.sum((block_start[:, None] >= pends[None, :]).astype(jnp.int32), axis=1),
                          N_EXPERTS - 1)
    used = (pends[-1:] // tm).astype(jnp.int32)
    xs = _scatter_rows(dest, u, cap, tm=ROW_TILE)
    y = _experts(block_e, used, xs, w1, w3, w2, tm=tm, tf=896)
    return _combine(dest, h, meta, y, final_g, tm=ROW_TILE, final_norm=final_norm)


def kernel(x, mem, mem_norm, norm_mix, w_in, conv_w, conv_b, dt_bias, a_log, d_skip, sb_norm, ssd_norm, w_out, norm_cross, wq_x, wk_x, wv_x, wo_x, norm_ffn, ffn_w1, ffn_w3, ffn_w2, router, moe_w1, moe_w3, moe_w2, final_norm):
    b, s, d = x.shape
    depth = w_in.shape[0]
    n = b * s
    mlen = mem.shape[1]
    main_cols = 3 * SB_WIDTH + SSD_WIDTH + conv_w.shape[2]
    row = lambda v: v.astype(F32)[None, :]
    h = x.astype(F32).reshape(n, d)
    mem2 = mem.astype(F32).reshape(b * mlen, d)
    for i in range(depth):
        w_main = w_in[i, :, :main_cols].astype(BF16)
        w_dt = jnp.zeros((d, LANES), BF16).at[:, :SSD_HEADS].set(w_in[i, :, main_cols:].astype(BF16))
        proj, dt_raw = _norm_proj(h, row(norm_mix[i]), w_main, w_dt, tm=ROW_TILE)
        proj3 = proj.reshape(b, s, main_cols)
        attn = _sb_attention(proj3, blk=SB_BLOCK, heads_per_step=SB_HEADS_PER_STEP)
        ssd = _ssd(proj3, dt_raw.reshape(b, s, LANES), conv_w[i], conv_b[i], dt_bias[i], a_log[i],
                   d_skip[i], chunk=SSD_CHUNK)
        w_kv = jnp.concatenate([wk_x[i], wv_x[i]], axis=1).astype(BF16)
        kv = _norm_proj(mem2, row(mem_norm), w_kv, tm=ROW_TILE)
        h = _mix_cross(h.reshape(b, s, d), attn, ssd, row(sb_norm[i]), row(ssd_norm[i]),
                       w_out[i].astype(BF16), row(norm_cross[i]), wq_x[i].astype(BF16),
                       kv.reshape(b, mlen, 2 * d), wo_x[i].astype(BF16), tm=ROW_TILE).reshape(n, d)
        j = i // 2
        last = i == depth - 1
        if i % 2 == 0:
            h = _ffn(h, row(norm_ffn[i]), ffn_w1[j].astype(BF16), ffn_w3[j].astype(BF16),
                     ffn_w2[j].astype(BF16), tm=ROW_TILE, tf=1408)
            if last:
                h = _final_norm(h, row(final_norm), tm=ROW_TILE)
        else:
            h = _moe(h, row(norm_ffn[i]), router[j], moe_w1[j].astype(BF16), moe_w3[j].astype(BF16),
                     moe_w2[j].astype(BF16), row(final_norm), final_norm=last)
    return h.reshape(b, s, d)
```

```python
import functools

import jax
import jax.numpy as jnp
from jax import lax
from jax.experimental import pallas as pl
from jax.experimental.pallas import tpu as pltpu

F32 = jnp.float32
BF16 = jnp.bfloat16

EPS = 1e-6
HEAD_DIM = 64
SB_WIDTH = 512
SSD_WIDTH = 512
SSD_HEADS = 8
SSD_STATE = 128
SSD_GROUPS = 2
CONV_WIDTH = 4
X_HEADS = 4
N_EXPERTS = 8
LANES = 128
SUBLANES = 8
VMEM_LIMIT_BYTES = 56 * 1024 * 1024
UNDERFLOW_LOG = 105.0

SB_BLOCK = 256
SB_HEADS_PER_STEP = 8
SSD_CHUNK = 256
ROW_TILE = 512
MOE_TILE = 512
HIDDEN_CHUNK = 1024
DMA_UNROLL = 8


def _params(*sem):
    return pltpu.CompilerParams(dimension_semantics=sem, vmem_limit_bytes=VMEM_LIMIT_BYTES)


def _rms(x, g):
    return x * lax.rsqrt(jnp.mean(x * x, axis=-1, keepdims=True) + EPS) * g


def _silu(x):
    return x / (1.0 + jnp.exp(-x))


def _dot(a, b):
    return jnp.dot(a, b, preferred_element_type=F32)


def _dot_nt(a, b):
    return lax.dot_general(a, b, (((1,), (1,)), ((), ())), preferred_element_type=F32)


def _dot_tn(a, b):
    return lax.dot_general(a, b, (((0,), (0,)), ((), ())), preferred_element_type=F32)


def _split2(x):
    hi = x.astype(BF16)
    lo = (x - hi.astype(F32)).astype(BF16)
    return hi, lo


def _split3(x):
    hi = x.astype(BF16)
    r = x - hi.astype(F32)
    mid = r.astype(BF16)
    lo = (r - mid.astype(F32)).astype(BF16)
    return hi, mid, lo


def _norm_proj_kernel(x_ref, g_ref, w_ref, *rest, col_chunk, with_aux):
    if with_aux:
        waux_ref, o_ref, oaux_ref = rest
    else:
        (o_ref,) = rest
    u = _rms(x_ref[...], g_ref[...]).astype(BF16)
    for c in range(o_ref.shape[1] // col_chunk):
        sl = slice(c * col_chunk, (c + 1) * col_chunk)
        o_ref[:, sl] = _dot(u, w_ref[:, sl]).astype(o_ref.dtype)
    if with_aux:
        oaux_ref[...] = _dot(u, waux_ref[...])


def _norm_proj(x, g, w, w_aux=None, *, tm):
    n, d = x.shape
    m = w.shape[1]
    with_aux = w_aux is not None
    in_specs = [
        pl.BlockSpec((tm, d), lambda i: (i, 0)),
        pl.BlockSpec((1, d), lambda i: (0, 0)),
        pl.BlockSpec((d, m), lambda i: (0, 0)),
    ]
    out_shape = [jax.ShapeDtypeStruct((n, m), BF16)]
    out_specs = [pl.BlockSpec((tm, m), lambda i: (i, 0))]
    args = [x, g, w]
    if with_aux:
        in_specs.append(pl.BlockSpec((d, LANES), lambda i: (0, 0)))
        out_shape.append(jax.ShapeDtypeStruct((n, LANES), F32))
        out_specs.append(pl.BlockSpec((tm, LANES), lambda i: (i, 0)))
        args.append(w_aux)
    res = pl.pallas_call(
        functools.partial(_norm_proj_kernel, col_chunk=512, with_aux=with_aux),
        grid=(n // tm,),
        in_specs=in_specs,
        out_specs=out_specs,
        out_shape=out_shape,
        compiler_params=_params("parallel"),
        name="norm_proj_aux" if with_aux else "norm_proj",
    )(*args)
    return res if with_aux else res[0]


def _sb_kernel(q_ref, k_ref, v_ref, o_ref, carry_ref, acc_ref, *, blk, scale):
    nblk = q_ref.shape[0] // blk
    nheads = q_ref.shape[1] // HEAD_DIM
    rows = lax.broadcasted_iota(jnp.int32, (blk, blk), 0)
    cols = lax.broadcasted_iota(jnp.int32, (blk, blk), 1)
    suffix_ones = (rows >= cols).astype(BF16)
    before = cols < rows
    heads = [slice(hh * HEAD_DIM, (hh + 1) * HEAD_DIM) for hh in range(nheads)]

    def all_heads(qrows, kb, diag):
        ks = pl.ds(pl.multiple_of(kb * blk, blk), blk)
        zs = [_dot_nt(q_ref[qrows, lanes] * scale, k_ref[ks, lanes]) for lanes in heads]
        fails = [jnp.maximum(z, 0.0) + jnp.log(1.0 + jnp.exp(-jnp.abs(z))) for z in zs]
        if diag:
            fails = [jnp.where(before, f, 0.0) for f in fails]
        sufs = [_dot(f.astype(BF16), suffix_ones) for f in fails]
        ws = [jnp.exp(z - suf) for z, suf in zip(zs, sufs)]
        if diag:
            ws = [jnp.where(before, w, 0.0) for w in ws]
        pvs = [_dot(w.astype(BF16), v_ref[ks, lanes]) for w, lanes in zip(ws, heads)]
        low = None
        for hh in range(nheads):
            if diag:
                carry = sufs[hh][:, 0:1]
                acc_ref[hh] = pvs[hh]
            else:
                carry = carry_ref[hh]
                acc_ref[hh] += jnp.exp(-carry) * pvs[hh]
                carry = carry + sufs[hh][:, 0:1]
            carry_ref[hh] = carry
            low = jnp.min(carry) if low is None else jnp.minimum(low, jnp.min(carry))
        return low

    def q_block(i, _):
        qrows = pl.ds(pl.multiple_of(i * blk, blk), blk)
        low = all_heads(qrows, i, True)

        def live(st):
            return jnp.logical_and(st[0] >= 0, st[1] < UNDERFLOW_LOG)

        def body(st):
            return st[0] - 1, all_heads(qrows, st[0], False)

        lax.while_loop(live, body, (i - 1, low))
        o_ref[qrows, :] = jnp.concatenate([acc_ref[hh] for hh in range(nheads)], axis=-1)
        return 0

    lax.fori_loop(0, nblk, q_block, 0)


def _sb_attention(proj3, *, blk, heads_per_step):
    b, s, _ = proj3.shape
    width = heads_per_step * HEAD_DIM
    groups = SB_WIDTH // width
    spec = lambda off: pl.BlockSpec((None, s, width), lambda bi, hg: (bi, 0, off + hg))
    return pl.pallas_call(
        functools.partial(_sb_kernel, blk=blk, scale=HEAD_DIM ** -0.5),
        grid=(b, groups),
        in_specs=[spec(0), spec(groups), spec(2 * groups)],
        out_specs=pl.BlockSpec((None, s, width), lambda bi, hg: (bi, 0, hg)),
        out_shape=jax.ShapeDtypeStruct((b, s, SB_WIDTH), F32),
        scratch_shapes=[pltpu.VMEM((heads_per_step, blk, 1), F32),
                        pltpu.VMEM((heads_per_step, blk, HEAD_DIM), F32)],
        compiler_params=_params("parallel", "parallel"),
        name="stick_breaking",
    )(proj3, proj3, proj3)


def _ssd_kernel(xbc_ref, z_ref, dt_ref, cw_ref, cb_ref, dtb_ref, alog_ref, dskip_ref, expand_ref,
                o_ref, xpad_ref, state_ref, *, chunk):
    c = pl.program_id(1)
    gn = SSD_GROUPS * SSD_STATE
    half = SSD_WIDTH // SSD_GROUPS

    @pl.when(c == 0)
    def _():
        xpad_ref[0:SUBLANES, :] = jnp.zeros((SUBLANES, xpad_ref.shape[1]), F32)
        state_ref[...] = jnp.zeros(state_ref.shape, F32)

    @pl.when(c > 0)
    def _():
        xpad_ref[0:SUBLANES, :] = xpad_ref[chunk:chunk + SUBLANES, :]

    xpad_ref[SUBLANES:SUBLANES + chunk, :] = xbc_ref[...].astype(F32)
    conv = cb_ref[...]
    for tap in range(CONV_WIDTH):
        start = SUBLANES - (CONV_WIDTH - 1) + tap
        conv = conv + cw_ref[tap:tap + 1, :] * xpad_ref[start:start + chunk, :]
    act = _silu(conv)
    xs = act[:, :SSD_WIDTH]
    bm = act[:, SSD_WIDTH:SSD_WIDTH + gn].astype(BF16)
    cm = act[:, SSD_WIDTH + gn:].astype(BF16)

    raw = dt_ref[...] + dtb_ref[...]
    dt8 = jnp.maximum(raw, 0.0) + jnp.log(1.0 + jnp.exp(-jnp.abs(raw)))
    expand = expand_ref[...]
    d1, d2, d3 = _split3(dt8)
    dt = _dot(d1, expand) + _dot(d2, expand) + _dot(d3, expand)
    a = -jnp.exp(alog_ref[...])
    rows = lax.broadcasted_iota(jnp.int32, (chunk, chunk), 0)
    cols = lax.broadcasted_iota(jnp.int32, (chunk, chunk), 1)
    causal = rows >= cols
    prefix_ones = causal.astype(BF16)
    l1, l2, l3 = _split3(dt * a)
    la = _dot(prefix_ones, l1) + _dot(prefix_ones, l2) + _dot(prefix_ones, l3)
    la_t = jnp.transpose(la)
    xdt = xs * dt
    xdt_b = xdt.astype(BF16)

    y_parts = []
    for g in range(SSD_GROUPS):
        cb = _dot_nt(cm[:, g * SSD_STATE:(g + 1) * SSD_STATE], bm[:, g * SSD_STATE:(g + 1) * SSD_STATE])
        for r in range(SSD_HEADS // SSD_GROUPS):
            lane0 = (g * (SSD_HEADS // SSD_GROUPS) + r) * HEAD_DIM
            diff = la[:, lane0:lane0 + 1] - la_t[lane0:lane0 + 1, :]
            m = cb * jnp.exp(jnp.where(causal, diff, -jnp.inf))
            y_parts.append(_dot(m.astype(BF16), xdt_b[:, lane0:lane0 + HEAD_DIM]))
    y = jnp.concatenate(y_parts, axis=-1)

    state = state_ref[...]
    state_b = state.astype(BF16)
    y_off = jnp.concatenate(
        [_dot(cm[:, g * SSD_STATE:(g + 1) * SSD_STATE], state_b[:, g * half:(g + 1) * half])
         for g in range(SSD_GROUPS)], axis=-1)
    y = y + y_off * jnp.exp(la)
    la_last = la[chunk - 1:chunk, :]
    xdec = (xdt * jnp.exp(la_last - la)).astype(BF16)
    upd = jnp.concatenate(
        [_dot_tn(bm[:, g * SSD_STATE:(g + 1) * SSD_STATE], xdec[:, g * half:(g + 1) * half])
         for g in range(SSD_GROUPS)], axis=-1)
    state_ref[...] = state * jnp.exp(la_last) + upd

    y = y + xs * dskip_ref[...]
    o_ref[...] = y * _silu(z_ref[...].astype(F32))


def _ssd(proj3, dt3, conv_w, conv_b, dt_bias, a_log, d_skip, *, chunk):
    b, s, _ = proj3.shape
    conv_ch = conv_w.shape[1]
    rep = lambda v: jnp.repeat(v.astype(F32), HEAD_DIM)[None, :]
    dtb = jnp.zeros((1, LANES), F32).at[0, :SSD_HEADS].set(dt_bias.astype(F32))
    expand = (jnp.arange(LANES)[:, None] == (jnp.arange(SSD_WIDTH) // HEAD_DIM)[None, :]).astype(BF16)
    z_blk = (3 * SB_WIDTH) // SSD_WIDTH
    xbc_blk = (3 * SB_WIDTH + SSD_WIDTH) // conv_ch
    const = lambda shape: pl.BlockSpec(shape, lambda bi, ci: (0,) * len(shape))
    return pl.pallas_call(
        functools.partial(_ssd_kernel, chunk=chunk),
        grid=(b, s // chunk),
        in_specs=[
            pl.BlockSpec((None, chunk, conv_ch), lambda bi, ci: (bi, ci, xbc_blk)),
            pl.BlockSpec((None, chunk, SSD_WIDTH), lambda bi, ci: (bi, ci, z_blk)),
            pl.BlockSpec((None, chunk, LANES), lambda bi, ci: (bi, ci, 0)),
            const((CONV_WIDTH, conv_ch)), const((1, conv_ch)), const((1, LANES)),
            const((1, SSD_WIDTH)), const((1, SSD_WIDTH)), const((LANES, SSD_WIDTH)),
        ],
        out_specs=pl.BlockSpec((None, chunk, SSD_WIDTH), lambda bi, ci: (bi, ci, 0)),
        out_shape=jax.ShapeDtypeStruct((b, s, SSD_WIDTH), F32),
        scratch_shapes=[pltpu.VMEM((chunk + SUBLANES, conv_ch), F32),
                        pltpu.VMEM((SSD_STATE, SSD_WIDTH), F32)],
        compiler_params=_params("parallel", "arbitrary"),
        name="ssd",
    )(proj3, proj3, dt3, conv_w.astype(F32), conv_b.astype(F32)[None, :], dtb,
      rep(a_log), rep(d_skip), expand)


def _mix_cross_kernel(h_ref, attn_ref, ssd_ref, sbg_ref, ssdg_ref, wout_ref, ng_ref, wq_ref,
                      kv_ref, wo_ref, o_ref, *, scale):
    d = h_ref.shape[1]
    hd = d // X_HEADS
    a = _rms(attn_ref[...], sbg_ref[...]).astype(BF16)
    s = _rms(ssd_ref[...], ssdg_ref[...]).astype(BF16)
    h = h_ref[...] + _dot(a, wout_ref[0:SB_WIDTH, :]) + _dot(s, wout_ref[SB_WIDTH:, :])
    u = _rms(h, ng_ref[...]).astype(BF16)
    q = _dot(u, wq_ref[...]).astype(BF16)
    heads = []
    for hh in range(X_HEADS):
        lanes = slice(hh * hd, (hh + 1) * hd)
        sc = _dot_nt(q[:, lanes], kv_ref[:, lanes]) * scale
        p = jnp.exp(sc - jnp.max(sc, axis=-1, keepdims=True))
        p = p / jnp.sum(p, axis=-1, keepdims=True)
        heads.append(_dot(p.astype(BF16), kv_ref[:, d + hh * hd:d + (hh + 1) * hd]).astype(BF16))
    o_ref[...] = h + _dot(jnp.concatenate(heads, axis=-1), wo_ref[...])


def _mix_cross(h3, attn3, ssd3, sb_g, ssd_g, w_out, norm_g, wq, kv3, wo, *, tm):
    b, s, d = h3.shape
    mem = kv3.shape[1]
    const = lambda shape: pl.BlockSpec(shape, lambda bi, ti: (0,) * len(shape))
    tile = lambda w: pl.BlockSpec((None, tm, w), lambda bi, ti: (bi, ti, 0))
    return pl.pallas_call(
        functools.partial(_mix_cross_kernel, scale=(d // X_HEADS) ** -0.5),
        grid=(b, s // tm),
        in_specs=[tile(d), tile(SB_WIDTH), tile(SSD_WIDTH), const((1, SB_WIDTH)), const((1, SSD_WIDTH)),
                  const((SB_WIDTH + SSD_WIDTH, d)), const((1, d)), const((d, d)),
                  pl.BlockSpec((None, mem, 2 * d), lambda bi, ti: (bi, 0, 0)), const((d, d))],
        out_specs=tile(d),
        out_shape=jax.ShapeDtypeStruct((b, s, d), F32),
        compiler_params=_params("parallel", "parallel"),
        name="mix_cross",
    )(h3, attn3, ssd3, sb_g, ssd_g, w_out, norm_g, wq, kv3, wo)


def _swiglu_hidden(x, w1_ref, w3_ref, act_ref):
    ff = w1_ref.shape[1]
    for c0 in range(0, ff, HIDDEN_CHUNK):
        sl = slice(c0, min(c0 + HIDDEN_CHUNK, ff))
        act_ref[:, sl] = (_silu(_dot(x, w1_ref[:, sl])) * _dot(x, w3_ref[:, sl])).astype(BF16)


def _ffn_kernel(h_ref, g_ref, w1_ref, w3_ref, w2_ref, o_ref, act_ref):
    h = h_ref[...]
    _swiglu_hidden(_rms(h, g_ref[...]).astype(BF16), w1_ref, w3_ref, act_ref)
    o_ref[...] = h + _dot(act_ref[...], w2_ref[...])


def _resident(shape, index_map):
    return pl.BlockSpec(shape, index_map, pipeline_mode=pl.Buffered(1))


def _ffn(h, g, w1, w3, w2, *, tm):
    n, d = h.shape
    ff = w1.shape[1]
    return pl.pallas_call(
        _ffn_kernel,
        grid=(n // tm,),
        in_specs=[pl.BlockSpec((tm, d), lambda i: (i, 0)),
                  pl.BlockSpec((1, d), lambda i: (0, 0)),
                  _resident((d, ff), lambda i: (0, 0)),
                  _resident((d, ff), lambda i: (0, 0)),
                  _resident((ff, d), lambda i: (0, 0))],
        out_specs=pl.BlockSpec((tm, d), lambda i: (i, 0)),
        out_shape=jax.ShapeDtypeStruct((n, d), F32),
        scratch_shapes=[pltpu.VMEM((tm, ff), BF16)],
        compiler_params=_params("parallel"),
        name="ffn",
    )(h, g, w1, w3, w2)


def _router_kernel(h_ref, g_ref, wr_ref, u_ref, meta_ref, cnt_ref, run_ref):
    i = pl.program_id(0)
    tm = h_ref.shape[0]

    @pl.when(i == 0)
    def _():
        run_ref[...] = jnp.zeros(run_ref.shape, F32)

    u = _rms(h_ref[...], g_ref[...])
    u_ref[...] = u
    logits = jnp.dot(u, wr_ref[...], preferred_element_type=F32, precision=lax.Precision.HIGHEST)
    lane = lax.broadcasted_iota(jnp.int32, (tm, LANES), 1)
    logits = jnp.where(lane < N_EXPERTS, logits, -jnp.inf)
    m1 = jnp.max(logits, axis=-1, keepdims=True)
    e1 = jnp.min(jnp.where(logits == m1, lane, LANES), axis=-1, keepdims=True)
    rest = jnp.where(lane == e1, -jnp.inf, logits)
    m2 = jnp.max(rest, axis=-1, keepdims=True)
    e2 = jnp.min(jnp.where(rest == m2, lane, LANES), axis=-1, keepdims=True)
    t = jnp.exp(m2 - m1)
    g1 = 1.0 / (1.0 + t)
    g2 = t / (1.0 + t)
    oh1 = (lane == e1).astype(F32)
    oh2 = (lane == e2).astype(F32)
    both = oh1 + oh2
    rows = lax.broadcasted_iota(jnp.int32, (tm, tm), 0)
    cols = lax.broadcasted_iota(jnp.int32, (tm, tm), 1)
    earlier = (cols < rows).astype(BF16)
    before = _dot(earlier, both.astype(BF16)) + run_ref[...]
    r1 = jnp.sum(before * oh1, axis=-1, keepdims=True)
    r2 = jnp.sum(before * oh2, axis=-1, keepdims=True)
    meta = jnp.where(lane == 0, e1.astype(F32), 0.0)
    meta = jnp.where(lane == 1, e2.astype(F32), meta)
    meta = jnp.where(lane == 2, g1, meta)
    meta = jnp.where(lane == 3, g2, meta)
    meta = jnp.where(lane == 4, r1, meta)
    meta = jnp.where(lane == 5, r2, meta)
    meta_ref[...] = meta
    run_ref[...] += jnp.sum(both, axis=0, keepdims=True)
    cnt_ref[...] = run_ref[...]


def _router(h, g, wr, *, tm):
    n, d = h.shape
    return pl.pallas_call(
        _router_kernel,
        grid=(n // tm,),
        in_specs=[pl.BlockSpec((tm, d), lambda i: (i, 0)),
                  pl.BlockSpec((1, d), lambda i: (0, 0)),
                  pl.BlockSpec((d, LANES), lambda i: (0, 0))],
        out_specs=[pl.BlockSpec((tm, d), lambda i: (i, 0)),
                   pl.BlockSpec((tm, LANES), lambda i: (i, 0)),
                   pl.BlockSpec((1, LANES), lambda i: (0, 0))],
        out_shape=[jax.ShapeDtypeStruct((n, d), F32),
                   jax.ShapeDtypeStruct((n, LANES), F32),
                   jax.ShapeDtypeStruct((1, LANES), F32)],
        scratch_shapes=[pltpu.VMEM((1, LANES), F32)],
        compiler_params=_params("arbitrary"),
        name="router",
    )(h, g, wr)


def _scatter_kernel(dest_ref, u_ref, init_ref, xs_ref, sems):
    del init_ref
    tm = u_ref.shape[0]

    def start(r, _):
        for k in range(2):
            pltpu.make_async_copy(u_ref.at[pl.ds(r, 1)],
                                  xs_ref.at[pl.ds(dest_ref[0, 2 * r + k], 1)],
                                  sems.at[k]).start(priority=k)
        return 0

    lax.fori_loop(0, tm, start, 0, unroll=DMA_UNROLL)
    for k in range(2):
        pltpu.make_async_copy(u_ref, xs_ref.at[pl.ds(0, tm)], sems.at[k]).wait()


def _dest_spec(tm):
    return pl.BlockSpec((None, 1, 2 * tm), lambda i: (i, 0, 0), memory_space=pltpu.SMEM)


def _scatter_rows(dest, u, cap, *, tm):
    n, d = u.shape
    init = jnp.zeros((cap, d), u.dtype)
    return pl.pallas_call(
        _scatter_kernel,
        grid=(n // tm,),
        in_specs=[_dest_spec(tm),
                  pl.BlockSpec((tm, d), lambda i: (i, 0)),
                  pl.BlockSpec(memory_space=pl.ANY)],
        out_specs=pl.BlockSpec(memory_space=pl.ANY),
        out_shape=jax.ShapeDtypeStruct((cap, d), u.dtype),
        scratch_shapes=[pltpu.SemaphoreType.DMA((2,))],
        input_output_aliases={2: 0},
        compiler_params=_params("arbitrary"),
        name="moe_scatter",
    )(dest, u, init)


def _expert_kernel(be_ref, used_ref, x_ref, w1_ref, w3_ref, w2_ref, o_ref, act_ref):
    del be_ref
    i = pl.program_id(0)

    @pl.when(i < used_ref[0])
    def _():
        _swiglu_hidden(x_ref[...].astype(BF16), w1_ref, w3_ref, act_ref)
        o_ref[...] = _dot(act_ref[...], w2_ref[...])

    @pl.when(i >= used_ref[0])
    def _():
        o_ref[...] = jnp.zeros(o_ref.shape, F32)


def _experts(block_e, used, xs, w1, w3, w2, *, tm):
    cap, d = xs.shape
    ff = w1.shape[2]
    return pl.pallas_call(
        _expert_kernel,
        grid_spec=pltpu.PrefetchScalarGridSpec(
            num_scalar_prefetch=2,
            grid=(cap // tm,),
            in_specs=[pl.BlockSpec((tm, d), lambda i, be, nu: (i, 0)),
                      _resident((None, d, ff), lambda i, be, nu: (be[i], 0, 0)),
                      _resident((None, d, ff), lambda i, be, nu: (be[i], 0, 0)),
                      _resident((None, ff, d), lambda i, be, nu: (be[i], 0, 0))],
            out_specs=pl.BlockSpec((tm, d), lambda i, be, nu: (i, 0)),
            scratch_shapes=[pltpu.VMEM((tm, ff), BF16)],
        ),
        out_shape=jax.ShapeDtypeStruct((cap, d), F32),
        compiler_params=_params("arbitrary"),
        name="moe_experts",
    )(block_e, used, xs, w1, w3, w2)


def _combine_kernel(dest_ref, h_ref, meta_ref, y_ref, fg_ref, o_ref, buf_ref, sems, *, final_norm):
    tm = h_ref.shape[0]

    def start(r, _):
        for k in range(2):
            pltpu.make_async_copy(y_ref.at[pl.ds(dest_ref[0, 2 * r + k], 1)],
                                  buf_ref.at[k, pl.ds(r, 1)], sems.at[k]).start(priority=k)
        return 0

    lax.fori_loop(0, tm, start, 0, unroll=DMA_UNROLL)
    for k in range(2):
        pltpu.make_async_copy(y_ref.at[pl.ds(0, tm)], buf_ref.at[k], sems.at[k]).wait()
    meta = meta_ref[...]
    out = h_ref[...] + meta[:, 2:3] * buf_ref[0] + meta[:, 3:4] * buf_ref[1]
    if final_norm:
        out = _rms(out, fg_ref[...])
    o_ref[...] = out


def _combine(dest, h, meta, y, fg, *, tm, final_norm):
    n, d = h.shape
    return pl.pallas_call(
        functools.partial(_combine_kernel, final_norm=final_norm),
        grid=(n // tm,),
        in_specs=[_dest_spec(tm),
                  pl.BlockSpec((tm, d), lambda i: (i, 0)),
                  pl.BlockSpec((tm, LANES), lambda i: (i, 0)),
                  pl.BlockSpec(memory_space=pl.ANY),
                  pl.BlockSpec((1, d), lambda i: (0, 0))],
        out_specs=pl.BlockSpec((tm, d), lambda i: (i, 0)),
        out_shape=jax.ShapeDtypeStruct((n, d), F32),
        scratch_shapes=[pltpu.VMEM((2, tm, d), F32), pltpu.SemaphoreType.DMA((2,))],
        compiler_params=_params("arbitrary"),
        name="moe_combine",
    )(dest, h, meta, y, fg)


def _final_norm_kernel(h_ref, g_ref, o_ref):
    o_ref[...] = _rms(h_ref[...], g_ref[...])


def _final_norm(h, g, *, tm):
    n, d = h.shape
    return pl.pallas_call(
        _final_norm_kernel,
        grid=(n // tm,),
        in_specs=[pl.BlockSpec((tm, d), lambda i: (i, 0)), pl.BlockSpec((1, d), lambda i: (0, 0))],
        out_specs=pl.BlockSpec((tm, d), lambda i: (i, 0)),
        out_shape=jax.ShapeDtypeStruct((n, d), F32),
        compiler_params=_params("parallel"),
        name="final_norm",
    )(h, g)


def _moe(h, g, router_w, w1, w3, w2, final_g, *, final_norm):
    n, d = h.shape
    tm = MOE_TILE
    wr = jnp.zeros((d, LANES), F32).at[:, :N_EXPERTS].set(router_w.astype(F32))
    u, meta, counts = _router(h, g, wr, tm=ROW_TILE)
    counts = counts[0, :N_EXPERTS].astype(jnp.int32)
    padded = (counts + tm - 1) // tm * tm
    pends = jnp.cumsum(padded)
    pstarts = pends - padded
    experts = meta[:, 0:2].astype(jnp.int32)
    dest = (pstarts[experts] + meta[:, 4:6].astype(jnp.int32)).reshape(n // ROW_TILE, 1, 2 * ROW_TILE)
    cap = (-(-2 * n // tm)) * tm + N_EXPERTS * tm
    block_start = jnp.arange(cap // tm, dtype=jnp.int32) * tm
    block_e = jnp.minimum(jnp.sum((block_start[:, None] >= pends[None, :]).astype(jnp.int32), axis=1),
                          N_EXPERTS - 1)
    used = (pends[-1:] // tm).astype(jnp.int32)
    xs = _scatter_rows(dest, u, cap, tm=ROW_TILE)
    y = _experts(block_e, used, xs, w1, w3, w2, tm=tm)
    return _combine(dest, h, meta, y, final_g, tm=ROW_TILE, final_norm=final_norm)


def kernel(x, mem, mem_norm, norm_mix, w_in, conv_w, conv_b, dt_bias, a_log, d_skip, sb_norm, ssd_norm, w_out, norm_cross, wq_x, wk_x, wv_x, wo_x, norm_ffn, ffn_w1, ffn_w3, ffn_w2, router, moe_w1, moe_w3, moe_w2, final_norm):
    b, s, d = x.shape
    depth = w_in.shape[0]
    n = b * s
    mlen = mem.shape[1]
    main_cols = 3 * SB_WIDTH + SSD_WIDTH + conv_w.shape[2]
    row = lambda v: v.astype(F32)[None, :]
    h = x.astype(F32).reshape(n, d)
    mem2 = mem.astype(F32).reshape(b * mlen, d)
    for i in range(depth):
        w_main = w_in[i, :, :main_cols].astype(BF16)
        w_dt = jnp.zeros((d, LANES), BF16).at[:, :SSD_HEADS].set(w_in[i, :, main_cols:].astype(BF16))
        proj, dt_raw = _norm_proj(h, row(norm_mix[i]), w_main, w_dt, tm=ROW_TILE)
        proj3 = proj.reshape(b, s, main_cols)
        attn = _sb_attention(proj3, blk=SB_BLOCK, heads_per_step=SB_HEADS_PER_STEP)
        ssd = _ssd(proj3, dt_raw.reshape(b, s, LANES), conv_w[i], conv_b[i], dt_bias[i], a_log[i],
                   d_skip[i], chunk=SSD_CHUNK)
        w_kv = jnp.concatenate([wk_x[i], wv_x[i]], axis=1).astype(BF16)
        kv = _norm_proj(mem2, row(mem_norm), w_kv, tm=ROW_TILE)
        h = _mix_cross(h.reshape(b, s, d), attn, ssd, row(sb_norm[i]), row(ssd_norm[i]),
                       w_out[i].astype(BF16), row(norm_cross[i]), wq_x[i].astype(BF16),
                       kv.reshape(b, mlen, 2 * d), wo_x[i].astype(BF16), tm=ROW_TILE).reshape(n, d)
        j = i // 2
        last = i == depth - 1
        if i % 2 == 0:
            h = _ffn(h, row(norm_ffn[i]), ffn_w1[j].astype(BF16), ffn_w3[j].astype(BF16),
                     ffn_w2[j].astype(BF16), tm=ROW_TILE)
            if last:
                h = _final_norm(h, row(final_norm), tm=ROW_TILE)
        else:
            h = _moe(h, row(norm_ffn[i]), router[j], moe_w1[j].astype(BF16), moe_w3[j].astype(BF16),
                     moe_w2[j].astype(BF16), row(final_norm), final_norm=last)
    return h.reshape(b, s, d)
```

```python
import functools

import jax
import jax.numpy as jnp
from jax import lax
from jax.experimental import pallas as pl
from jax.experimental.pallas import tpu as pltpu

F32 = jnp.float32
BF16 = jnp.bfloat16

EPS = 1e-6
HEAD_DIM = 64
SB_WIDTH = 512
SSD_WIDTH = 512
SSD_HEADS = 8
SSD_STATE = 128
SSD_GROUPS = 2
CONV_WIDTH = 4
X_HEADS = 4
N_EXPERTS = 8
LANES = 128
SUBLANES = 8
VMEM_LIMIT_BYTES = 56 * 1024 * 1024
UNDERFLOW_LOG = 105.0

SB_BLOCK = 256
SB_HEADS_PER_STEP = 8
SSD_CHUNK = 256
ROW_TILE = 512
MOE_TILE = 512
HIDDEN_CHUNK = 1024
DMA_UNROLL = 8


def _params(*sem):
    return pltpu.CompilerParams(dimension_semantics=sem, vmem_limit_bytes=VMEM_LIMIT_BYTES)


def _rms(x, g):
    return x * lax.rsqrt(jnp.mean(x * x, axis=-1, keepdims=True) + EPS) * g


def _silu(x):
    return x / (1.0 + jnp.exp(-x))


def _dot(a, b):
    return jnp.dot(a, b, preferred_element_type=F32)


def _dot_nt(a, b):
    return lax.dot_general(a, b, (((1,), (1,)), ((), ())), preferred_element_type=F32)


def _dot_tn(a, b):
    return lax.dot_general(a, b, (((0,), (0,)), ((), ())), preferred_element_type=F32)


def _split2(x):
    hi = x.astype(BF16)
    lo = (x - hi.astype(F32)).astype(BF16)
    return hi, lo


def _split3(x):
    hi = x.astype(BF16)
    r = x - hi.astype(F32)
    mid = r.astype(BF16)
    lo = (r - mid.astype(F32)).astype(BF16)
    return hi, mid, lo


def _norm_proj_kernel(x_ref, g_ref, w_ref, *rest, col_chunk, with_aux):
    if with_aux:
        waux_ref, o_ref, oaux_ref = rest
    else:
        (o_ref,) = rest
    u = _rms(x_ref[...], g_ref[...]).astype(BF16)
    for c in range(o_ref.shape[1] // col_chunk):
        sl = slice(c * col_chunk, (c + 1) * col_chunk)
        o_ref[:, sl] = _dot(u, w_ref[:, sl]).astype(o_ref.dtype)
    if with_aux:
        oaux_ref[...] = _dot(u, waux_ref[...])


def _norm_proj(x, g, w, w_aux=None, *, tm):
    n, d = x.shape
    m = w.shape[1]
    with_aux = w_aux is not None
    in_specs = [
        pl.BlockSpec((tm, d), lambda i: (i, 0)),
        pl.BlockSpec((1, d), lambda i: (0, 0)),
        pl.BlockSpec((d, m), lambda i: (0, 0)),
    ]
    out_shape = [jax.ShapeDtypeStruct((n, m), BF16)]
    out_specs = [pl.BlockSpec((tm, m), lambda i: (i, 0))]
    args = [x, g, w]
    if with_aux:
        in_specs.append(pl.BlockSpec((d, LANES), lambda i: (0, 0)))
        out_shape.append(jax.ShapeDtypeStruct((n, LANES), F32))
        out_specs.append(pl.BlockSpec((tm, LANES), lambda i: (i, 0)))
        args.append(w_aux)
    res = pl.pallas_call(
        functools.partial(_norm_proj_kernel, col_chunk=512, with_aux=with_aux),
        grid=(n // tm,),
        in_specs=in_specs,
        out_specs=out_specs,
        out_shape=out_shape,
        compiler_params=_params("parallel"),
        name="norm_proj_aux" if with_aux else "norm_proj",
    )(*args)
    return res if with_aux else res[0]


def _sb_kernel(q_ref, k_ref, v_ref, o_ref, carry_ref, acc_ref, *, blk, scale):
    nblk = q_ref.shape[0] // blk
    nheads = q_ref.shape[1] // HEAD_DIM
    rows = lax.broadcasted_iota(jnp.int32, (blk, blk), 0)
    cols = lax.broadcasted_iota(jnp.int32, (blk, blk), 1)
    suffix_ones = (rows >= cols).astype(BF16)
    before = cols < rows
    heads = [slice(hh * HEAD_DIM, (hh + 1) * HEAD_DIM) for hh in range(nheads)]

    def all_heads(qrows, kb, diag):
        ks = pl.ds(pl.multiple_of(kb * blk, blk), blk)
        zs = [_dot_nt(q_ref[qrows, lanes] * scale, k_ref[ks, lanes]) for lanes in heads]
        fails = [jnp.maximum(z, 0.0) + jnp.log(1.0 + jnp.exp(-jnp.abs(z))) for z in zs]
        if diag:
            fails = [jnp.where(before, f, 0.0) for f in fails]
        sufs = [_dot(f.astype(BF16), suffix_ones) for f in fails]
        ws = [jnp.exp(z - suf) for z, suf in zip(zs, sufs)]
        if diag:
            ws = [jnp.where(before, w, 0.0) for w in ws]
        pvs = [_dot(w.astype(BF16), v_ref[ks, lanes]) for w, lanes in zip(ws, heads)]
        low = None
        for hh in range(nheads):
            if diag:
                carry = sufs[hh][:, 0:1]
                acc_ref[hh] = pvs[hh]
            else:
                carry = carry_ref[hh]
                acc_ref[hh] += jnp.exp(-carry) * pvs[hh]
                carry = carry + sufs[hh][:, 0:1]
            carry_ref[hh] = carry
            low = jnp.min(carry) if low is None else jnp.minimum(low, jnp.min(carry))
        return low

    def q_block(i, _):
        qrows = pl.ds(pl.multiple_of(i * blk, blk), blk)
        low = all_heads(qrows, i, True)

        def live(st):
            return jnp.logical_and(st[0] >= 0, st[1] < UNDERFLOW_LOG)

        def body(st):
            return st[0] - 1, all_heads(qrows, st[0], False)

        lax.while_loop(live, body, (i - 1, low))
        o_ref[qrows, :] = jnp.concatenate([acc_ref[hh] for hh in range(nheads)], axis=-1)
        return 0

    lax.fori_loop(0, nblk, q_block, 0)


def _sb_attention(proj3, *, blk, heads_per_step):
    b, s, _ = proj3.shape
    width = heads_per_step * HEAD_DIM
    groups = SB_WIDTH // width
    spec = lambda off: pl.BlockSpec((None, s, width), lambda bi, hg: (bi, 0, off + hg))
    return pl.pallas_call(
        functools.partial(_sb_kernel, blk=blk, scale=HEAD_DIM ** -0.5),
        grid=(b, groups),
        in_specs=[spec(0), spec(groups), spec(2 * groups)],
        out_specs=pl.BlockSpec((None, s, width), lambda bi, hg: (bi, 0, hg)),
        out_shape=jax.ShapeDtypeStruct((b, s, SB_WIDTH), F32),
        scratch_shapes=[pltpu.VMEM((heads_per_step, blk, 1), F32),
                        pltpu.VMEM((heads_per_step, blk, HEAD_DIM), F32)],
        compiler_params=_params("parallel", "parallel"),
        name="stick_breaking",
    )(proj3, proj3, proj3)


def _ssd_kernel(xbc_ref, z_ref, dt_ref, cw_ref, cb_ref, dtb_ref, alog_ref, dskip_ref, expand_ref,
                o_ref, tail_ref, state_ref, *, chunk):
    c = pl.program_id(1)
    gn = SSD_GROUPS * SSD_STATE
    half = SSD_WIDTH // SSD_GROUPS

    @pl.when(c == 0)
    def _():
        tail_ref[...] = jnp.zeros(tail_ref.shape, F32)
        state_ref[...] = jnp.zeros(state_ref.shape, F32)

    rows = lax.broadcasted_iota(jnp.int32, (chunk, chunk), 0)
    cols = lax.broadcasted_iota(jnp.int32, (chunk, chunk), 1)
    causal = rows >= cols
    prefix_ones = causal.astype(BF16)

    x_b = xbc_ref[...]
    conv = cb_ref[...] + cw_ref[CONV_WIDTH - 1:CONV_WIDTH, :] * x_b.astype(F32)
    tail = tail_ref[...]
    head_row = lax.broadcasted_iota(jnp.int32, (SUBLANES, 1), 0)
    fix = jnp.zeros(tail.shape, F32)
    for back in range(1, CONV_WIDTH):
        w_tap = cw_ref[CONV_WIDTH - 1 - back:CONV_WIDTH - back, :]
        conv = conv + w_tap * _dot((rows - cols == back).astype(BF16), x_b)
        fix = fix + jnp.where(head_row < back, w_tap * pltpu.roll(tail, back, 0), 0.0)
    conv = jnp.concatenate([conv[:SUBLANES] + fix, conv[SUBLANES:]], axis=0)
    tail_ref[...] = x_b[chunk - SUBLANES:, :].astype(F32)
    act = _silu(conv)
    xs = act[:, :SSD_WIDTH]
    bm = act[:, SSD_WIDTH:SSD_WIDTH + gn].astype(BF16)
    cm = act[:, SSD_WIDTH + gn:].astype(BF16)

    raw = dt_ref[...] + dtb_ref[...]
    dt8 = jnp.maximum(raw, 0.0) + jnp.log(1.0 + jnp.exp(-jnp.abs(raw)))
    expand = expand_ref[...]
    l1, l2, l3 = _split3(dt8 * -jnp.exp(alog_ref[...]))
    la8 = _dot(prefix_ones, l1) + _dot(prefix_ones, l2) + _dot(prefix_ones, l3)
    d1, d2, d3 = _split3(dt8)
    dt = _dot(d1, expand) + _dot(d2, expand) + _dot(d3, expand)
    e1, e2, e3 = _split3(la8)
    la = _dot(e1, expand) + _dot(e2, expand) + _dot(e3, expand)
    la_t = jnp.transpose(la)
    xdt = xs * dt
    xdt_b = xdt.astype(BF16)

    y_parts = []
    for g in range(SSD_GROUPS):
        cb = _dot_nt(cm[:, g * SSD_STATE:(g + 1) * SSD_STATE], bm[:, g * SSD_STATE:(g + 1) * SSD_STATE])
        for r in range(SSD_HEADS // SSD_GROUPS):
            lane0 = (g * (SSD_HEADS // SSD_GROUPS) + r) * HEAD_DIM
            diff = la[:, lane0:lane0 + 1] - la_t[lane0:lane0 + 1, :]
            m = cb * jnp.exp(jnp.where(causal, diff, -jnp.inf))
            y_parts.append(_dot(m.astype(BF16), xdt_b[:, lane0:lane0 + HEAD_DIM]))
    y = jnp.concatenate(y_parts, axis=-1)

    state = state_ref[...]
    state_b = state.astype(BF16)
    y_off = jnp.concatenate(
        [_dot(cm[:, g * SSD_STATE:(g + 1) * SSD_STATE], state_b[:, g * half:(g + 1) * half])
         for g in range(SSD_GROUPS)], axis=-1)
    y = y + y_off * jnp.exp(la)
    la_last = la[chunk - 1:chunk, :]
    xdec = (xdt * jnp.exp(la_last - la)).astype(BF16)
    upd = jnp.concatenate(
        [_dot_tn(bm[:, g * SSD_STATE:(g + 1) * SSD_STATE], xdec[:, g * half:(g + 1) * half])
         for g in range(SSD_GROUPS)], axis=-1)
    state_ref[...] = state * jnp.exp(la_last) + upd

    y = y + xs * dskip_ref[...]
    o_ref[...] = y * _silu(z_ref[...].astype(F32))


def _ssd(proj3, dt3, conv_w, conv_b, dt_bias, a_log, d_skip, *, chunk):
    b, s, _ = proj3.shape
    conv_ch = conv_w.shape[1]
    rep = lambda v: jnp.repeat(v.astype(F32), HEAD_DIM)[None, :]
    pad = lambda v: jnp.zeros((1, LANES), F32).at[0, :SSD_HEADS].set(v.astype(F32))
    expand = (jnp.arange(LANES)[:, None] == (jnp.arange(SSD_WIDTH) // HEAD_DIM)[None, :]).astype(BF16)
    z_blk = (3 * SB_WIDTH) // SSD_WIDTH
    xbc_blk = (3 * SB_WIDTH + SSD_WIDTH) // conv_ch
    const = lambda shape: pl.BlockSpec(shape, lambda bi, ci: (0,) * len(shape))
    return pl.pallas_call(
        functools.partial(_ssd_kernel, chunk=chunk),
        grid=(b, s // chunk),
        in_specs=[
            pl.BlockSpec((None, chunk, conv_ch), lambda bi, ci: (bi, ci, xbc_blk)),
            pl.BlockSpec((None, chunk, SSD_WIDTH), lambda bi, ci: (bi, ci, z_blk)),
            pl.BlockSpec((None, chunk, LANES), lambda bi, ci: (bi, ci, 0)),
            const((CONV_WIDTH, conv_ch)), const((1, conv_ch)), const((1, LANES)),
            const((1, LANES)), const((1, SSD_WIDTH)), const((LANES, SSD_WIDTH)),
        ],
        out_specs=pl.BlockSpec((None, chunk, SSD_WIDTH), lambda bi, ci: (bi, ci, 0)),
        out_shape=jax.ShapeDtypeStruct((b, s, SSD_WIDTH), F32),
        scratch_shapes=[pltpu.VMEM((SUBLANES, conv_ch), F32),
                        pltpu.VMEM((SSD_STATE, SSD_WIDTH), F32)],
        compiler_params=_params("parallel", "arbitrary"),
        name="ssd",
    )(proj3, proj3, dt3, conv_w.astype(F32), conv_b.astype(F32)[None, :], pad(dt_bias),
      pad(a_log), rep(d_skip), expand)


def _mix_cross_kernel(h_ref, attn_ref, ssd_ref, sbg_ref, ssdg_ref, wout_ref, ng_ref, wq_ref,
                      kv_ref, wo_ref, o_ref, *, scale):
    d = h_ref.shape[1]
    hd = d // X_HEADS
    a = _rms(attn_ref[...], sbg_ref[...]).astype(BF16)
    s = _rms(ssd_ref[...], ssdg_ref[...]).astype(BF16)
    h = h_ref[...] + _dot(a, wout_ref[0:SB_WIDTH, :]) + _dot(s, wout_ref[SB_WIDTH:, :])
    u = _rms(h, ng_ref[...]).astype(BF16)
    q = _dot(u, wq_ref[...]).astype(BF16)
    heads = []
    for hh in range(X_HEADS):
        lanes = slice(hh * hd, (hh + 1) * hd)
        sc = _dot_nt(q[:, lanes], kv_ref[:, lanes]) * scale
        p = jnp.exp(sc - jnp.max(sc, axis=-1, keepdims=True))
        p = p / jnp.sum(p, axis=-1, keepdims=True)
        heads.append(_dot(p.astype(BF16), kv_ref[:, d + hh * hd:d + (hh + 1) * hd]).astype(BF16))
    o_ref[...] = h + _dot(jnp.concatenate(heads, axis=-1), wo_ref[...])


def _mix_cross(h3, attn3, ssd3, sb_g, ssd_g, w_out, norm_g, wq, kv3, wo, *, tm):
    b, s, d = h3.shape
    mem = kv3.shape[1]
    const = lambda shape: pl.BlockSpec(shape, lambda bi, ti: (0,) * len(shape))
    tile = lambda w: pl.BlockSpec((None, tm, w), lambda bi, ti: (bi, ti, 0))
    return pl.pallas_call(
        functools.partial(_mix_cross_kernel, scale=(d // X_HEADS) ** -0.5),
        grid=(b, s // tm),
        in_specs=[tile(d), tile(SB_WIDTH), tile(SSD_WIDTH), const((1, SB_WIDTH)), const((1, SSD_WIDTH)),
                  const((SB_WIDTH + SSD_WIDTH, d)), const((1, d)), const((d, d)),
                  pl.BlockSpec((None, mem, 2 * d), lambda bi, ti: (bi, 0, 0)), const((d, d))],
        out_specs=tile(d),
        out_shape=jax.ShapeDtypeStruct((b, s, d), F32),
        compiler_params=_params("parallel", "parallel"),
        name="mix_cross",
    )(h3, attn3, ssd3, sb_g, ssd_g, w_out, norm_g, wq, kv3, wo)


def _swiglu_hidden(x, w1_ref, w3_ref, act_ref):
    ff = w1_ref.shape[1]
    for c0 in range(0, ff, HIDDEN_CHUNK):
        sl = slice(c0, min(c0 + HIDDEN_CHUNK, ff))
        act_ref[:, sl] = (_silu(_dot(x, w1_ref[:, sl])) * _dot(x, w3_ref[:, sl])).astype(BF16)


def _ffn_kernel(h_ref, g_ref, w1_ref, w3_ref, w2_ref, o_ref, act_ref):
    h = h_ref[...]
    _swiglu_hidden(_rms(h, g_ref[...]).astype(BF16), w1_ref, w3_ref, act_ref)
    o_ref[...] = h + _dot(act_ref[...], w2_ref[...])


def _resident(shape, index_map):
    return pl.BlockSpec(shape, index_map, pipeline_mode=pl.Buffered(1))


def _ffn(h, g, w1, w3, w2, *, tm):
    n, d = h.shape
    ff = w1.shape[1]
    return pl.pallas_call(
        _ffn_kernel,
        grid=(n // tm,),
        in_specs=[pl.BlockSpec((tm, d), lambda i: (i, 0)),
                  pl.BlockSpec((1, d), lambda i: (0, 0)),
                  _resident((d, ff), lambda i: (0, 0)),
                  _resident((d, ff), lambda i: (0, 0)),
                  _resident((ff, d), lambda i: (0, 0))],
        out_specs=pl.BlockSpec((tm, d), lambda i: (i, 0)),
        out_shape=jax.ShapeDtypeStruct((n, d), F32),
        scratch_shapes=[pltpu.VMEM((tm, ff), BF16)],
        compiler_params=_params("parallel"),
        name="ffn",
    )(h, g, w1, w3, w2)


def _router_kernel(h_ref, g_ref, wr_ref, u_ref, meta_ref, cnt_ref, run_ref):
    i = pl.program_id(0)
    tm = h_ref.shape[0]

    @pl.when(i == 0)
    def _():
        run_ref[...] = jnp.zeros(run_ref.shape, F32)

    u = _rms(h_ref[...], g_ref[...])
    u_ref[...] = u
    uh, ul = _split2(u)
    part = _dot(uh, wr_ref[...])
    logits = part[:, :LANES] + part[:, LANES:] + _dot(ul, wr_ref[:, :LANES])
    lane = lax.broadcasted_iota(jnp.int32, (tm, LANES), 1)
    logits = jnp.where(lane < N_EXPERTS, logits, -jnp.inf)
    m1 = jnp.max(logits, axis=-1, keepdims=True)
    e1 = jnp.min(jnp.where(logits == m1, lane, LANES), axis=-1, keepdims=True)
    rest = jnp.where(lane == e1, -jnp.inf, logits)
    m2 = jnp.max(rest, axis=-1, keepdims=True)
    e2 = jnp.min(jnp.where(rest == m2, lane, LANES), axis=-1, keepdims=True)
    t = jnp.exp(m2 - m1)
    g1 = 1.0 / (1.0 + t)
    g2 = t / (1.0 + t)
    oh1 = (lane == e1).astype(F32)
    oh2 = (lane == e2).astype(F32)
    both = oh1 + oh2
    rows = lax.broadcasted_iota(jnp.int32, (tm, tm), 0)
    cols = lax.broadcasted_iota(jnp.int32, (tm, tm), 1)
    earlier = (cols < rows).astype(BF16)
    before = _dot(earlier, both.astype(BF16)) + run_ref[...]
    r1 = jnp.sum(before * oh1, axis=-1, keepdims=True)
    r2 = jnp.sum(before * oh2, axis=-1, keepdims=True)
    meta = jnp.where(lane == 0, e1.astype(F32), 0.0)
    meta = jnp.where(lane == 1, e2.astype(F32), meta)
    meta = jnp.where(lane == 2, g1, meta)
    meta = jnp.where(lane == 3, g2, meta)
    meta = jnp.where(lane == 4, r1, meta)
    meta = jnp.where(lane == 5, r2, meta)
    meta_ref[...] = meta
    run_ref[...] += jnp.sum(both, axis=0, keepdims=True)
    cnt_ref[...] = run_ref[...]


def _router(h, g, wr, *, tm):
    n, d = h.shape
    return pl.pallas_call(
        _router_kernel,
        grid=(n // tm,),
        in_specs=[pl.BlockSpec((tm, d), lambda i: (i, 0)),
                  pl.BlockSpec((1, d), lambda i: (0, 0)),
                  pl.BlockSpec((d, 2 * LANES), lambda i: (0, 0))],
        out_specs=[pl.BlockSpec((tm, d), lambda i: (i, 0)),
                   pl.BlockSpec((tm, LANES), lambda i: (i, 0)),
                   pl.BlockSpec((1, LANES), lambda i: (0, 0))],
        out_shape=[jax.ShapeDtypeStruct((n, d), F32),
                   jax.ShapeDtypeStruct((n, LANES), F32),
                   jax.ShapeDtypeStruct((1, LANES), F32)],
        scratch_shapes=[pltpu.VMEM((1, LANES), F32)],
        compiler_params=_params("arbitrary"),
        name="router",
    )(h, g, wr)


def _scatter_kernel(dest_ref, fill_ref, u_ref, xs_ref, zero_ref, sems):
    tm = u_ref.shape[0]

    @pl.when(pl.program_id(0) == 0)
    def _():
        zero_ref[...] = jnp.zeros(zero_ref.shape, zero_ref.dtype)
        for e in range(N_EXPERTS):
            first = pl.multiple_of(fill_ref[e], MOE_TILE)
            fill = pltpu.make_async_copy(zero_ref, xs_ref.at[pl.ds(first, MOE_TILE)], sems.at[2])
            fill.start()
            fill.wait()

    def start(r, _):
        for k in range(2):
            pltpu.make_async_copy(u_ref.at[pl.ds(r, 1)],
                                  xs_ref.at[pl.ds(dest_ref[0, 2 * r + k], 1)],
                                  sems.at[k]).start(priority=k)
        return 0

    lax.fori_loop(0, tm, start, 0, unroll=DMA_UNROLL)
    for k in range(2):
        pltpu.make_async_copy(u_ref, xs_ref.at[pl.ds(0, tm)], sems.at[k]).wait()


def _dest_spec(tm):
    return pl.BlockSpec((None, 1, 2 * tm), lambda i: (i, 0, 0), memory_space=pltpu.SMEM)


def _scatter_rows(dest, fill_start, u, cap, *, tm):
    n, d = u.shape
    return pl.pallas_call(
        _scatter_kernel,
        grid=(n // tm,),
        in_specs=[_dest_spec(tm),
                  pl.BlockSpec(memory_space=pltpu.SMEM),
                  pl.BlockSpec((tm, d), lambda i: (i, 0))],
        out_specs=pl.BlockSpec(memory_space=pl.ANY),
        out_shape=jax.ShapeDtypeStruct((cap, d), u.dtype),
        scratch_shapes=[pltpu.VMEM((MOE_TILE, d), u.dtype), pltpu.SemaphoreType.DMA((3,))],
        compiler_params=_params("arbitrary"),
        name="moe_scatter",
    )(dest, fill_start, u)


def _expert_kernel(be_ref, used_ref, x_ref, w1_ref, w3_ref, w2_ref, o_ref, act_ref):
    del be_ref
    i = pl.program_id(0)

    @pl.when(i < used_ref[0])
    def _():
        _swiglu_hidden(x_ref[...].astype(BF16), w1_ref, w3_ref, act_ref)
        o_ref[...] = _dot(act_ref[...], w2_ref[...])

    @pl.when(i >= used_ref[0])
    def _():
        o_ref[...] = jnp.zeros(o_ref.shape, F32)


def _experts(block_e, used, xs, w1, w3, w2, *, tm):
    cap, d = xs.shape
    ff = w1.shape[2]
    return pl.pallas_call(
        _expert_kernel,
        grid_spec=pltpu.PrefetchScalarGridSpec(
            num_scalar_prefetch=2,
            grid=(cap // tm,),
            in_specs=[pl.BlockSpec((tm, d), lambda i, be, nu: (jnp.minimum(i, nu[0] - 1), 0)),
                      _resident((None, d, ff), lambda i, be, nu: (be[i], 0, 0)),
                      _resident((None, d, ff), lambda i, be, nu: (be[i], 0, 0)),
                      _resident((None, ff, d), lambda i, be, nu: (be[i], 0, 0))],
            out_specs=pl.BlockSpec((tm, d), lambda i, be, nu: (i, 0)),
            scratch_shapes=[pltpu.VMEM((tm, ff), BF16)],
        ),
        out_shape=jax.ShapeDtypeStruct((cap, d), F32),
        compiler_params=_params("arbitrary"),
        name="moe_experts",
    )(block_e, used, xs, w1, w3, w2)


def _combine_kernel(dest_ref, h_ref, meta_ref, y_ref, fg_ref, o_ref, buf_ref, sems, *, final_norm):
    tm = h_ref.shape[0]

    def start(r, _):
        for k in range(2):
            pltpu.make_async_copy(y_ref.at[pl.ds(dest_ref[0, 2 * r + k], 1)],
                                  buf_ref.at[k, pl.ds(r, 1)], sems.at[k]).start(priority=k)
        return 0

    lax.fori_loop(0, tm, start, 0, unroll=DMA_UNROLL)
    for k in range(2):
        pltpu.make_async_copy(y_ref.at[pl.ds(0, tm)], buf_ref.at[k], sems.at[k]).wait()
    meta = meta_ref[...]
    out = h_ref[...] + meta[:, 2:3] * buf_ref[0] + meta[:, 3:4] * buf_ref[1]
    if final_norm:
        out = _rms(out, fg_ref[...])
    o_ref[...] = out


def _combine(dest, h, meta, y, fg, *, tm, final_norm):
    n, d = h.shape
    return pl.pallas_call(
        functools.partial(_combine_kernel, final_norm=final_norm),
        grid=(n // tm,),
        in_specs=[_dest_spec(tm),
                  pl.BlockSpec((tm, d), lambda i: (i, 0)),
                  pl.BlockSpec((tm, LANES), lambda i: (i, 0)),
                  pl.BlockSpec(memory_space=pl.ANY),
                  pl.BlockSpec((1, d), lambda i: (0, 0))],
        out_specs=pl.BlockSpec((tm, d), lambda i: (i, 0)),
        out_shape=jax.ShapeDtypeStruct((n, d), F32),
        scratch_shapes=[pltpu.VMEM((2, tm, d), F32), pltpu.SemaphoreType.DMA((2,))],
        compiler_params=_params("arbitrary"),
        name="moe_combine",
    )(dest, h, meta, y, fg)


def _final_norm_kernel(h_ref, g_ref, o_ref):
    o_ref[...] = _rms(h_ref[...], g_ref[...])


def _final_norm(h, g, *, tm):
    n, d = h.shape
    return pl.pallas_call(
        _final_norm_kernel,
        grid=(n // tm,),
        in_specs=[pl.BlockSpec((tm, d), lambda i: (i, 0)), pl.BlockSpec((1, d), lambda i: (0, 0))],
        out_specs=pl.BlockSpec((tm, d), lambda i: (i, 0)),
        out_shape=jax.ShapeDtypeStruct((n, d), F32),
        compiler_params=_params("parallel"),
        name="final_norm",
    )(h, g)


def _moe(h, g, router_w, w1, w3, w2, final_g, *, final_norm):
    n, d = h.shape
    tm = MOE_TILE
    wr = jnp.zeros((d, LANES), F32).at[:, :N_EXPERTS].set(router_w.astype(F32))
    u, meta, counts = _router(h, g, jnp.concatenate(_split2(wr), axis=1), tm=ROW_TILE)
    counts = counts[0, :N_EXPERTS].astype(jnp.int32)
    padded = (counts + tm - 1) // tm * tm
    pends = jnp.cumsum(padded)
    pstarts = pends - padded
    experts = meta[:, 0:2].astype(jnp.int32)
    dest = (pstarts[experts] + meta[:, 4:6].astype(jnp.int32)).reshape(n // ROW_TILE, 1, 2 * ROW_TILE)
    cap = (-(-2 * n // tm)) * tm + N_EXPERTS * tm
    block_start = jnp.arange(cap // tm, dtype=jnp.int32) * tm
    block_e = jnp.minimum(jnp.sum((block_start[:, None] >= pends[None, :]).astype(jnp.int32), axis=1),
                          N_EXPERTS - 1)
    used = (pends[-1:] // tm).astype(jnp.int32)
    xs = _scatter_rows(dest, jnp.maximum(pends - tm, 0), u, cap, tm=ROW_TILE)
    y = _experts(block_e, used, xs, w1, w3, w2, tm=tm)
    return _combine(dest, h, meta, y, final_g, tm=ROW_TILE, final_norm=final_norm)


def kernel(x, mem, mem_norm, norm_mix, w_in, conv_w, conv_b, dt_bias, a_log, d_skip, sb_norm, ssd_norm, w_out, norm_cross, wq_x, wk_x, wv_x, wo_x, norm_ffn, ffn_w1, ffn_w3, ffn_w2, router, moe_w1, moe_w3, moe_w2, final_norm):
    b, s, d = x.shape
    depth = w_in.shape[0]
    n = b * s
    mlen = mem.shape[1]
    main_cols = 3 * SB_WIDTH + SSD_WIDTH + conv_w.shape[2]
    row = lambda v: v.astype(F32)[None, :]
    h = x.astype(F32).reshape(n, d)
    mem2 = mem.astype(F32).reshape(b * mlen, d)
    for i in range(depth):
        w_main = w_in[i, :, :main_cols].astype(BF16)
        w_dt = jnp.zeros((d, LANES), BF16).at[:, :SSD_HEADS].set(w_in[i, :, main_cols:].astype(BF16))
        proj, dt_raw = _norm_proj(h, row(norm_mix[i]), w_main, w_dt, tm=ROW_TILE)
        proj3 = proj.reshape(b, s, main_cols)
        attn = _sb_attention(proj3, blk=SB_BLOCK, heads_per_step=SB_HEADS_PER_STEP)
        ssd = _ssd(proj3, dt_raw.reshape(b, s, LANES), conv_w[i], conv_b[i], dt_bias[i], a_log[i],
                   d_skip[i], chunk=SSD_CHUNK)
        w_kv = jnp.concatenate([wk_x[i], wv_x[i]], axis=1).astype(BF16)
        kv = _norm_proj(mem2, row(mem_norm), w_kv, tm=ROW_TILE)
        h = _mix_cross(h.reshape(b, s, d), attn, ssd, row(sb_norm[i]), row(ssd_norm[i]),
                       w_out[i].astype(BF16), row(norm_cross[i]), wq_x[i].astype(BF16),
                       kv.reshape(b, mlen, 2 * d), wo_x[i].astype(BF16), tm=ROW_TILE).reshape(n, d)
        j = i // 2
        last = i == depth - 1
        if i % 2 == 0:
            h = _ffn(h, row(norm_ffn[i]), ffn_w1[j].astype(BF16), ffn_w3[j].astype(BF16),
                     ffn_w2[j].astype(BF16), tm=ROW_TILE)
            if last:
                h = _final_norm(h, row(final_norm), tm=ROW_TILE)
        else:
            h = _moe(h, row(norm_ffn[i]), router[j], moe_w1[j].astype(BF16), moe_w3[j].astype(BF16),
                     moe_w2[j].astype(BF16), row(final_norm), final_norm=last)
    return h.reshape(b, s, d)
```

```python
import functools

import jax
import jax.numpy as jnp
from jax import lax
from jax.experimental import pallas as pl
from jax.experimental.pallas import tpu as pltpu

F32 = jnp.float32
BF16 = jnp.bfloat16

EPS = 1e-6
HEAD_DIM = 64
SB_WIDTH = 512
SSD_WIDTH = 512
SSD_HEADS = 8
SSD_STATE = 128
SSD_GROUPS = 2
CONV_WIDTH = 4
X_HEADS = 4
N_EXPERTS = 8
LANES = 128
SUBLANES = 8
VMEM_LIMIT_BYTES = 56 * 1024 * 1024
UNDERFLOW_LOG = 105.0

SB_BLOCK = 256
SB_HEADS_PER_STEP = 8
SSD_CHUNK = 256
ROW_TILE = 512
MATMUL_TILE = 1024
MOE_TILE = 512
HIDDEN_CHUNK = 1024
DMA_UNROLL = 8


def _params(*sem):
    return pltpu.CompilerParams(dimension_semantics=sem, vmem_limit_bytes=VMEM_LIMIT_BYTES)


def _rms(x, g):
    return x * lax.rsqrt(jnp.mean(x * x, axis=-1, keepdims=True) + EPS) * g


def _silu(x):
    return x / (1.0 + jnp.exp(-x))


def _dot(a, b):
    return jnp.dot(a, b, preferred_element_type=F32)


def _dot_nt(a, b):
    return lax.dot_general(a, b, (((1,), (1,)), ((), ())), preferred_element_type=F32)


def _dot_tn(a, b):
    return lax.dot_general(a, b, (((0,), (0,)), ((), ())), preferred_element_type=F32)


def _split2(x):
    hi = x.astype(BF16)
    lo = (x - hi.astype(F32)).astype(BF16)
    return hi, lo


def _split3(x):
    hi = x.astype(BF16)
    r = x - hi.astype(F32)
    mid = r.astype(BF16)
    lo = (r - mid.astype(F32)).astype(BF16)
    return hi, mid, lo


def _norm_proj_kernel(x_ref, g_ref, w_ref, *rest, col_chunk, with_aux):
    if with_aux:
        waux_ref, o_ref, oaux_ref = rest
    else:
        (o_ref,) = rest
    u = _rms(x_ref[...], g_ref[...]).astype(BF16)
    for c in range(o_ref.shape[1] // col_chunk):
        sl = slice(c * col_chunk, (c + 1) * col_chunk)
        o_ref[:, sl] = _dot(u, w_ref[:, sl]).astype(o_ref.dtype)
    if with_aux:
        oaux_ref[...] = _dot(u, waux_ref[...])


def _norm_proj(x, g, w, w_aux=None, *, tm):
    n, d = x.shape
    m = w.shape[1]
    with_aux = w_aux is not None
    in_specs = [
        pl.BlockSpec((tm, d), lambda i: (i, 0)),
        pl.BlockSpec((1, d), lambda i: (0, 0)),
        _resident((d, m), lambda i: (0, 0)),
    ]
    out_shape = [jax.ShapeDtypeStruct((n, m), BF16)]
    out_specs = [pl.BlockSpec((tm, m), lambda i: (i, 0))]
    args = [x, g, w]
    if with_aux:
        in_specs.append(pl.BlockSpec((d, LANES), lambda i: (0, 0)))
        out_shape.append(jax.ShapeDtypeStruct((n, LANES), F32))
        out_specs.append(pl.BlockSpec((tm, LANES), lambda i: (i, 0)))
        args.append(w_aux)
    res = pl.pallas_call(
        functools.partial(_norm_proj_kernel, col_chunk=512, with_aux=with_aux),
        grid=(n // tm,),
        in_specs=in_specs,
        out_specs=out_specs,
        out_shape=out_shape,
        compiler_params=_params("parallel"),
        name="norm_proj_aux" if with_aux else "norm_proj",
    )(*args)
    return res if with_aux else res[0]


def _sb_kernel(q_ref, k_ref, v_ref, o_ref, carry_ref, acc_ref, *, blk, scale):
    nblk = q_ref.shape[0] // blk
    nheads = q_ref.shape[1] // HEAD_DIM
    rows = lax.broadcasted_iota(jnp.int32, (blk, blk), 0)
    cols = lax.broadcasted_iota(jnp.int32, (blk, blk), 1)
    suffix_ones = (rows >= cols).astype(BF16)
    before = cols < rows
    heads = [slice(hh * HEAD_DIM, (hh + 1) * HEAD_DIM) for hh in range(nheads)]

    def all_heads(qrows, kb, diag):
        ks = pl.ds(pl.multiple_of(kb * blk, blk), blk)
        zs = [_dot_nt(q_ref[qrows, lanes] * scale, k_ref[ks, lanes]) for lanes in heads]
        fails = [jnp.maximum(z, 0.0) + jnp.log(1.0 + jnp.exp(-jnp.abs(z))) for z in zs]
        if diag:
            fails = [jnp.where(before, f, 0.0) for f in fails]
        sufs = [_dot(f.astype(BF16), suffix_ones) for f in fails]
        ws = [jnp.exp(z - suf) for z, suf in zip(zs, sufs)]
        if diag:
            ws = [jnp.where(before, w, 0.0) for w in ws]
        pvs = [_dot(w.astype(BF16), v_ref[ks, lanes]) for w, lanes in zip(ws, heads)]
        low = None
        for hh in range(nheads):
            if diag:
                carry = sufs[hh][:, 0:1]
                acc_ref[hh] = pvs[hh]
            else:
                carry = carry_ref[hh]
                acc_ref[hh] += jnp.exp(-carry) * pvs[hh]
                carry = carry + sufs[hh][:, 0:1]
            carry_ref[hh] = carry
            low = jnp.min(carry) if low is None else jnp.minimum(low, jnp.min(carry))
        return low

    def q_block(i, _):
        qrows = pl.ds(pl.multiple_of(i * blk, blk), blk)
        low = all_heads(qrows, i, True)

        def live(st):
            return jnp.logical_and(st[0] >= 0, st[1] < UNDERFLOW_LOG)

        def body(st):
            return st[0] - 1, all_heads(qrows, st[0], False)

        lax.while_loop(live, body, (i - 1, low))
        o_ref[qrows, :] = jnp.concatenate([acc_ref[hh] for hh in range(nheads)], axis=-1)
        return 0

    lax.fori_loop(0, nblk, q_block, 0)


def _sb_attention(proj3, *, blk, heads_per_step):
    b, s, _ = proj3.shape
    width = heads_per_step * HEAD_DIM
    groups = SB_WIDTH // width
    spec = lambda off: pl.BlockSpec((None, s, width), lambda bi, hg: (bi, 0, off + hg))
    return pl.pallas_call(
        functools.partial(_sb_kernel, blk=blk, scale=HEAD_DIM ** -0.5),
        grid=(b, groups),
        in_specs=[spec(0), spec(groups), spec(2 * groups)],
        out_specs=pl.BlockSpec((None, s, width), lambda bi, hg: (bi, 0, hg)),
        out_shape=jax.ShapeDtypeStruct((b, s, SB_WIDTH), F32),
        scratch_shapes=[pltpu.VMEM((heads_per_step, blk, 1), F32),
                        pltpu.VMEM((heads_per_step, blk, HEAD_DIM), F32)],
        compiler_params=_params("parallel", "parallel"),
        name="stick_breaking",
    )(proj3, proj3, proj3)


def _ssd_kernel(xbc_ref, z_ref, dt_ref, cw_ref, cb_ref, dtb_ref, alog_ref, dskip_ref, expand_ref,
                o_ref, tail_ref, state_ref, *, chunk):
    c = pl.program_id(1)
    gn = SSD_GROUPS * SSD_STATE
    half = SSD_WIDTH // SSD_GROUPS

    @pl.when(c == 0)
    def _():
        tail_ref[...] = jnp.zeros(tail_ref.shape, F32)
        state_ref[...] = jnp.zeros(state_ref.shape, F32)

    rows = lax.broadcasted_iota(jnp.int32, (chunk, chunk), 0)
    cols = lax.broadcasted_iota(jnp.int32, (chunk, chunk), 1)
    causal = rows >= cols
    prefix_ones = causal.astype(BF16)

    x_b = xbc_ref[...]
    conv = cb_ref[...] + cw_ref[CONV_WIDTH - 1:CONV_WIDTH, :] * x_b.astype(F32)
    tail = tail_ref[...]
    head_row = lax.broadcasted_iota(jnp.int32, (SUBLANES, 1), 0)
    fix = jnp.zeros(tail.shape, F32)
    for back in range(1, CONV_WIDTH):
        w_tap = cw_ref[CONV_WIDTH - 1 - back:CONV_WIDTH - back, :]
        conv = conv + w_tap * _dot((rows - cols == back).astype(BF16), x_b)
        fix = fix + jnp.where(head_row < back, w_tap * pltpu.roll(tail, back, 0), 0.0)
    conv = jnp.concatenate([conv[:SUBLANES] + fix, conv[SUBLANES:]], axis=0)
    tail_ref[...] = x_b[chunk - SUBLANES:, :].astype(F32)
    act = _silu(conv)
    xs = act[:, :SSD_WIDTH]
    bm = act[:, SSD_WIDTH:SSD_WIDTH + gn].astype(BF16)
    cm = act[:, SSD_WIDTH + gn:].astype(BF16)

    raw = dt_ref[...] + dtb_ref[...]
    dt8 = jnp.maximum(raw, 0.0) + jnp.log(1.0 + jnp.exp(-jnp.abs(raw)))
    expand = expand_ref[...]
    l1, l2, l3 = _split3(dt8 * -jnp.exp(alog_ref[...]))
    la8 = _dot(prefix_ones, l1) + _dot(prefix_ones, l2) + _dot(prefix_ones, l3)
    d1, d2, d3 = _split3(dt8)
    dt = _dot(d1, expand) + _dot(d2, expand) + _dot(d3, expand)
    e1, e2, e3 = _split3(la8)
    la = _dot(e1, expand) + _dot(e2, expand) + _dot(e3, expand)
    la_t = jnp.transpose(la)
    xdt = xs * dt
    xdt_b = xdt.astype(BF16)

    y_parts = []
    for g in range(SSD_GROUPS):
        cb = _dot_nt(cm[:, g * SSD_STATE:(g + 1) * SSD_STATE], bm[:, g * SSD_STATE:(g + 1) * SSD_STATE])
        for r in range(SSD_HEADS // SSD_GROUPS):
            lane0 = (g * (SSD_HEADS // SSD_GROUPS) + r) * HEAD_DIM
            diff = la[:, lane0:lane0 + 1] - la_t[lane0:lane0 + 1, :]
            m = cb * jnp.exp(jnp.where(causal, diff, -jnp.inf))
            y_parts.append(_dot(m.astype(BF16), xdt_b[:, lane0:lane0 + HEAD_DIM]))
    y = jnp.concatenate(y_parts, axis=-1)

    state = state_ref[...]
    state_b = state.astype(BF16)
    y_off = jnp.concatenate(
        [_dot(cm[:, g * SSD_STATE:(g + 1) * SSD_STATE], state_b[:, g * half:(g + 1) * half])
         for g in range(SSD_GROUPS)], axis=-1)
    y = y + y_off * jnp.exp(la)
    la_last = la[chunk - 1:chunk, :]
    xdec = (xdt * jnp.exp(la_last - la)).astype(BF16)
    upd = jnp.concatenate(
        [_dot_tn(bm[:, g * SSD_STATE:(g + 1) * SSD_STATE], xdec[:, g * half:(g + 1) * half])
         for g in range(SSD_GROUPS)], axis=-1)
    state_ref[...] = state * jnp.exp(la_last) + upd

    y = y + xs * dskip_ref[...]
    o_ref[...] = y * _silu(z_ref[...].astype(F32))


def _ssd(proj3, dt3, conv_w, conv_b, dt_bias, a_log, d_skip, *, chunk):
    b, s, _ = proj3.shape
    conv_ch = conv_w.shape[1]
    rep = lambda v: jnp.repeat(v.astype(F32), HEAD_DIM)[None, :]
    pad = lambda v: jnp.zeros((1, LANES), F32).at[0, :SSD_HEADS].set(v.astype(F32))
    expand = (jnp.arange(LANES)[:, None] == (jnp.arange(SSD_WIDTH) // HEAD_DIM)[None, :]).astype(BF16)
    z_blk = (3 * SB_WIDTH) // SSD_WIDTH
    xbc_blk = (3 * SB_WIDTH + SSD_WIDTH) // conv_ch
    const = lambda shape: pl.BlockSpec(shape, lambda bi, ci: (0,) * len(shape))
    return pl.pallas_call(
        functools.partial(_ssd_kernel, chunk=chunk),
        grid=(b, s // chunk),
        in_specs=[
            pl.BlockSpec((None, chunk, conv_ch), lambda bi, ci: (bi, ci, xbc_blk)),
            pl.BlockSpec((None, chunk, SSD_WIDTH), lambda bi, ci: (bi, ci, z_blk)),
            pl.BlockSpec((None, chunk, LANES), lambda bi, ci: (bi, ci, 0)),
            const((CONV_WIDTH, conv_ch)), const((1, conv_ch)), const((1, LANES)),
            const((1, LANES)), const((1, SSD_WIDTH)), const((LANES, SSD_WIDTH)),
        ],
        out_specs=pl.BlockSpec((None, chunk, SSD_WIDTH), lambda bi, ci: (bi, ci, 0)),
        out_shape=jax.ShapeDtypeStruct((b, s, SSD_WIDTH), F32),
        scratch_shapes=[pltpu.VMEM((SUBLANES, conv_ch), F32),
                        pltpu.VMEM((SSD_STATE, SSD_WIDTH), F32)],
        compiler_params=_params("parallel", "arbitrary"),
        name="ssd",
    )(proj3, proj3, dt3, conv_w.astype(F32), conv_b.astype(F32)[None, :], pad(dt_bias),
      pad(a_log), rep(d_skip), expand)


def _mix_cross_kernel(h_ref, attn_ref, ssd_ref, sbg_ref, ssdg_ref, wout_ref, ng_ref, wq_ref,
                      kv_ref, wo_ref, o_ref, *, scale):
    d = h_ref.shape[1]
    hd = d // X_HEADS
    a = _rms(attn_ref[...], sbg_ref[...]).astype(BF16)
    s = _rms(ssd_ref[...], ssdg_ref[...]).astype(BF16)
    h = h_ref[...] + _dot(a, wout_ref[0:SB_WIDTH, :]) + _dot(s, wout_ref[SB_WIDTH:, :])
    u = _rms(h, ng_ref[...]).astype(BF16)
    q = _dot(u, wq_ref[...]).astype(BF16)
    heads = []
    for hh in range(X_HEADS):
        lanes = slice(hh * hd, (hh + 1) * hd)
        sc = _dot_nt(q[:, lanes], kv_ref[:, lanes]) * scale
        p = jnp.exp(sc - jnp.max(sc, axis=-1, keepdims=True))
        p = p / jnp.sum(p, axis=-1, keepdims=True)
        heads.append(_dot(p.astype(BF16), kv_ref[:, d + hh * hd:d + (hh + 1) * hd]).astype(BF16))
    o_ref[...] = h + _dot(jnp.concatenate(heads, axis=-1), wo_ref[...])


def _mix_cross(h3, attn3, ssd3, sb_g, ssd_g, w_out, norm_g, wq, kv3, wo, *, tm):
    b, s, d = h3.shape
    mem = kv3.shape[1]
    tm = min(tm, s)
    const = lambda shape: _resident(shape, lambda bi, ti: (0,) * len(shape))
    tile = lambda w: pl.BlockSpec((None, tm, w), lambda bi, ti: (bi, ti, 0))
    return pl.pallas_call(
        functools.partial(_mix_cross_kernel, scale=(d // X_HEADS) ** -0.5),
        grid=(b, s // tm),
        in_specs=[tile(d), tile(SB_WIDTH), tile(SSD_WIDTH), const((1, SB_WIDTH)), const((1, SSD_WIDTH)),
                  const((SB_WIDTH + SSD_WIDTH, d)), const((1, d)), const((d, d)),
                  pl.BlockSpec((None, mem, 2 * d), lambda bi, ti: (bi, 0, 0)), const((d, d))],
        out_specs=tile(d),
        out_shape=jax.ShapeDtypeStruct((b, s, d), F32),
        compiler_params=_params("parallel", "parallel"),
        name="mix_cross",
    )(h3, attn3, ssd3, sb_g, ssd_g, w_out, norm_g, wq, kv3, wo)


def _swiglu_hidden(x, w1_ref, w3_ref, act_ref):
    ff = w1_ref.shape[1]
    for c0 in range(0, ff, HIDDEN_CHUNK):
        sl = slice(c0, min(c0 + HIDDEN_CHUNK, ff))
        act_ref[:, sl] = (_silu(_dot(x, w1_ref[:, sl])) * _dot(x, w3_ref[:, sl])).astype(BF16)


def _ffn_kernel(h_ref, g_ref, w1_ref, w3_ref, w2_ref, o_ref, act_ref):
    h = h_ref[...]
    _swiglu_hidden(_rms(h, g_ref[...]).astype(BF16), w1_ref, w3_ref, act_ref)
    o_ref[...] = h + _dot(act_ref[...], w2_ref[...])


def _resident(shape, index_map):
    return pl.BlockSpec(shape, index_map, pipeline_mode=pl.Buffered(1))


def _ffn(h, g, w1, w3, w2, *, tm):
    n, d = h.shape
    ff = w1.shape[1]
    return pl.pallas_call(
        _ffn_kernel,
        grid=(n // tm,),
        in_specs=[pl.BlockSpec((tm, d), lambda i: (i, 0)),
                  pl.BlockSpec((1, d), lambda i: (0, 0)),
                  _resident((d, ff), lambda i: (0, 0)),
                  _resident((d, ff), lambda i: (0, 0)),
                  _resident((ff, d), lambda i: (0, 0))],
        out_specs=pl.BlockSpec((tm, d), lambda i: (i, 0)),
        out_shape=jax.ShapeDtypeStruct((n, d), F32),
        scratch_shapes=[pltpu.VMEM((tm, ff), BF16)],
        compiler_params=_params("parallel"),
        name="ffn",
    )(h, g, w1, w3, w2)


def _store_row_tiles(ref, x):
    parts = x.shape[1] // LANES
    for j in range(parts):
        ref[pl.ds(j, x.shape[0], stride=parts), :] = x[:, j * LANES:(j + 1) * LANES]


def _load_row_tiles(ref, rows, parts):
    return jnp.concatenate([ref[pl.ds(j, rows, stride=parts), :] for j in range(parts)], axis=-1)


def _router_kernel(h_ref, g_ref, wr_ref, u_ref, meta_ref, cnt_ref, run_ref):
    i = pl.program_id(0)
    tm = h_ref.shape[0]

    @pl.when(i == 0)
    def _():
        run_ref[...] = jnp.zeros(run_ref.shape, F32)

    u = _rms(h_ref[...], g_ref[...])
    _store_row_tiles(u_ref, u)
    uh, ul = _split2(u)
    part = _dot(uh, wr_ref[...])
    logits = part[:, :LANES] + part[:, LANES:] + _dot(ul, wr_ref[:, :LANES])
    lane = lax.broadcasted_iota(jnp.int32, (tm, LANES), 1)
    logits = jnp.where(lane < N_EXPERTS, logits, -jnp.inf)
    m1 = jnp.max(logits, axis=-1, keepdims=True)
    e1 = jnp.min(jnp.where(logits == m1, lane, LANES), axis=-1, keepdims=True)
    rest = jnp.where(lane == e1, -jnp.inf, logits)
    m2 = jnp.max(rest, axis=-1, keepdims=True)
    e2 = jnp.min(jnp.where(rest == m2, lane, LANES), axis=-1, keepdims=True)
    t = jnp.exp(m2 - m1)
    g1 = 1.0 / (1.0 + t)
    g2 = t / (1.0 + t)
    oh1 = (lane == e1).astype(F32)
    oh2 = (lane == e2).astype(F32)
    both = oh1 + oh2
    rows = lax.broadcasted_iota(jnp.int32, (tm, tm), 0)
    cols = lax.broadcasted_iota(jnp.int32, (tm, tm), 1)
    earlier = (cols < rows).astype(BF16)
    before = _dot(earlier, both.astype(BF16)) + run_ref[...]
    r1 = jnp.sum(before * oh1, axis=-1, keepdims=True)
    r2 = jnp.sum(before * oh2, axis=-1, keepdims=True)
    meta = jnp.where(lane == 0, e1.astype(F32), 0.0)
    meta = jnp.where(lane == 1, e2.astype(F32), meta)
    meta = jnp.where(lane == 2, g1, meta)
    meta = jnp.where(lane == 3, g2, meta)
    meta = jnp.where(lane == 4, r1, meta)
    meta = jnp.where(lane == 5, r2, meta)
    meta_ref[...] = meta
    run_ref[...] += jnp.sum(both, axis=0, keepdims=True)
    cnt_ref[...] = run_ref[...]


def _router(h, g, wr, *, tm):
    n, d = h.shape
    parts = d // LANES
    return pl.pallas_call(
        _router_kernel,
        grid=(n // tm,),
        in_specs=[pl.BlockSpec((tm, d), lambda i: (i, 0)),
                  pl.BlockSpec((1, d), lambda i: (0, 0)),
                  pl.BlockSpec((d, 2 * LANES), lambda i: (0, 0))],
        out_specs=[pl.BlockSpec((tm * parts, LANES), lambda i: (i, 0)),
                   pl.BlockSpec((tm, LANES), lambda i: (i, 0)),
                   pl.BlockSpec((1, LANES), lambda i: (0, 0))],
        out_shape=[jax.ShapeDtypeStruct((n * parts, LANES), F32),
                   jax.ShapeDtypeStruct((n, LANES), F32),
                   jax.ShapeDtypeStruct((1, LANES), F32)],
        scratch_shapes=[pltpu.VMEM((1, LANES), F32)],
        compiler_params=_params("arbitrary"),
        name="router",
    )(h, g, wr)


def _tile_rows(first_row, nrows, parts):
    return pl.ds(pl.multiple_of(first_row * parts, parts), nrows * parts)


def _scatter_kernel(dest_ref, fill_ref, u_ref, xs_ref, zero_ref, sems, *, parts):
    tm = u_ref.shape[0] // parts

    @pl.when(pl.program_id(0) == 0)
    def _():
        zero_ref[...] = jnp.zeros(zero_ref.shape, zero_ref.dtype)
        for e in range(N_EXPERTS):
            fill = pltpu.make_async_copy(zero_ref, xs_ref.at[_tile_rows(fill_ref[e], MOE_TILE, parts)],
                                         sems.at[2])
            fill.start()
            fill.wait()

    def start(r, _):
        for k in range(2):
            pltpu.make_async_copy(u_ref.at[_tile_rows(r, 1, parts)],
                                  xs_ref.at[_tile_rows(dest_ref[0, 2 * r + k], 1, parts)],
                                  sems.at[k]).start(priority=k)
        return 0

    lax.fori_loop(0, tm, start, 0, unroll=DMA_UNROLL)
    for k in range(2):
        pltpu.make_async_copy(u_ref, xs_ref.at[_tile_rows(0, tm, parts)], sems.at[k]).wait()


def _dest_spec(tm):
    return pl.BlockSpec((None, 1, 2 * tm), lambda i: (i, 0, 0), memory_space=pltpu.SMEM)


def _scatter_rows(dest, fill_start, u, cap, *, tm, parts):
    return pl.pallas_call(
        functools.partial(_scatter_kernel, parts=parts),
        grid=(u.shape[0] // (tm * parts),),
        in_specs=[_dest_spec(tm),
                  pl.BlockSpec(memory_space=pltpu.SMEM),
                  pl.BlockSpec((tm * parts, LANES), lambda i: (i, 0))],
        out_specs=pl.BlockSpec(memory_space=pl.ANY),
        out_shape=jax.ShapeDtypeStruct((cap * parts, LANES), u.dtype),
        scratch_shapes=[pltpu.VMEM((MOE_TILE * parts, LANES), u.dtype), pltpu.SemaphoreType.DMA((3,))],
        compiler_params=_params("arbitrary"),
        name="moe_scatter",
    )(dest, fill_start, u)


def _expert_kernel(be_ref, used_ref, x_ref, w1_ref, w3_ref, w2_ref, o_ref, act_ref):
    del be_ref
    i = pl.program_id(0)
    tm = act_ref.shape[0]
    parts = x_ref.shape[0] // tm

    @pl.when(i < used_ref[0])
    def _():
        _swiglu_hidden(_load_row_tiles(x_ref, tm, parts).astype(BF16), w1_ref, w3_ref, act_ref)
        _store_row_tiles(o_ref, _dot(act_ref[...], w2_ref[...]))

    @pl.when(i >= used_ref[0])
    def _():
        o_ref[...] = jnp.zeros(o_ref.shape, F32)


def _experts(block_e, used, xs, w1, w3, w2, *, tm):
    d, ff = w1.shape[1], w1.shape[2]
    parts = d // LANES
    blocks = xs.shape[0] // (tm * parts)
    return pl.pallas_call(
        _expert_kernel,
        grid_spec=pltpu.PrefetchScalarGridSpec(
            num_scalar_prefetch=2,
            grid=(blocks,),
            in_specs=[pl.BlockSpec((tm * parts, LANES), lambda i, be, nu: (jnp.minimum(i, nu[0] - 1), 0)),
                      _resident((None, d, ff), lambda i, be, nu: (be[i], 0, 0)),
                      _resident((None, d, ff), lambda i, be, nu: (be[i], 0, 0)),
                      _resident((None, ff, d), lambda i, be, nu: (be[i], 0, 0))],
            out_specs=pl.BlockSpec((tm * parts, LANES), lambda i, be, nu: (i, 0)),
            scratch_shapes=[pltpu.VMEM((tm, ff), BF16)],
        ),
        out_shape=jax.ShapeDtypeStruct(xs.shape, F32),
        compiler_params=_params("arbitrary"),
        name="moe_experts",
    )(block_e, used, xs, w1, w3, w2)


def _combine_kernel(dest_ref, h_ref, meta_ref, y_ref, fg_ref, o_ref, buf_ref, sems, *, final_norm):
    tm, d = h_ref.shape
    parts = d // LANES

    def start(r, _):
        for k in range(2):
            pltpu.make_async_copy(y_ref.at[_tile_rows(dest_ref[0, 2 * r + k], 1, parts)],
                                  buf_ref.at[k, _tile_rows(r, 1, parts)], sems.at[k]).start(priority=k)
        return 0

    lax.fori_loop(0, tm, start, 0, unroll=DMA_UNROLL)
    for k in range(2):
        pltpu.make_async_copy(y_ref.at[_tile_rows(0, tm, parts)], buf_ref.at[k], sems.at[k]).wait()
    meta = meta_ref[...]
    out = (h_ref[...] + meta[:, 2:3] * _load_row_tiles(buf_ref.at[0], tm, parts)
           + meta[:, 3:4] * _load_row_tiles(buf_ref.at[1], tm, parts))
    if final_norm:
        out = _rms(out, fg_ref[...])
    o_ref[...] = out


def _combine(dest, h, meta, y, fg, *, tm, final_norm):
    n, d = h.shape
    return pl.pallas_call(
        functools.partial(_combine_kernel, final_norm=final_norm),
        grid=(n // tm,),
        in_specs=[_dest_spec(tm),
                  pl.BlockSpec((tm, d), lambda i: (i, 0)),
                  pl.BlockSpec((tm, LANES), lambda i: (i, 0)),
                  pl.BlockSpec(memory_space=pl.ANY),
                  pl.BlockSpec((1, d), lambda i: (0, 0))],
        out_specs=pl.BlockSpec((tm, d), lambda i: (i, 0)),
        out_shape=jax.ShapeDtypeStruct((n, d), F32),
        scratch_shapes=[pltpu.VMEM((2, tm * (d // LANES), LANES), F32), pltpu.SemaphoreType.DMA((2,))],
        compiler_params=_params("arbitrary"),
        name="moe_combine",
    )(dest, h, meta, y, fg)


def _final_norm_kernel(h_ref, g_ref, o_ref):
    o_ref[...] = _rms(h_ref[...], g_ref[...])


def _final_norm(h, g, *, tm):
    n, d = h.shape
    return pl.pallas_call(
        _final_norm_kernel,
        grid=(n // tm,),
        in_specs=[pl.BlockSpec((tm, d), lambda i: (i, 0)), pl.BlockSpec((1, d), lambda i: (0, 0))],
        out_specs=pl.BlockSpec((tm, d), lambda i: (i, 0)),
        out_shape=jax.ShapeDtypeStruct((n, d), F32),
        compiler_params=_params("parallel"),
        name="final_norm",
    )(h, g)


def _moe(h, g, router_w, w1, w3, w2, final_g, *, final_norm):
    n, d = h.shape
    assert (d // LANES) % SUBLANES == 0, "row-tile layout needs whole (8, 128) tiles per token row"
    tm = MOE_TILE
    wr = jnp.zeros((d, LANES), F32).at[:, :N_EXPERTS].set(router_w.astype(F32))
    u, meta, counts = _router(h, g, jnp.concatenate(_split2(wr), axis=1), tm=ROW_TILE)
    counts = counts[0, :N_EXPERTS].astype(jnp.int32)
    padded = (counts + tm - 1) // tm * tm
    pends = jnp.cumsum(padded)
    pstarts = pends - padded
    experts = meta[:, 0:2].astype(jnp.int32)
    dest = (pstarts[experts] + meta[:, 4:6].astype(jnp.int32)).reshape(n // ROW_TILE, 1, 2 * ROW_TILE)
    cap = (-(-2 * n // tm)) * tm + N_EXPERTS * tm
    block_start = jnp.arange(cap // tm, dtype=jnp.int32) * tm
    block_e = jnp.minimum(jnp.sum((block_start[:, None] >= pends[None, :]).astype(jnp.int32), axis=1),
                          N_EXPERTS - 1)
    used = (pends[-1:] // tm).astype(jnp.int32)
    xs = _scatter_rows(dest, jnp.maximum(pends - tm, 0), u, cap, tm=ROW_TILE, parts=d // LANES)
    y = _experts(block_e, used, xs, w1, w3, w2, tm=tm)
    return _combine(dest, h, meta, y, final_g, tm=ROW_TILE, final_norm=final_norm)


def kernel(x, mem, mem_norm, norm_mix, w_in, conv_w, conv_b, dt_bias, a_log, d_skip, sb_norm, ssd_norm, w_out, norm_cross, wq_x, wk_x, wv_x, wo_x, norm_ffn, ffn_w1, ffn_w3, ffn_w2, router, moe_w1, moe_w3, moe_w2, final_norm):
    b, s, d = x.shape
    depth = w_in.shape[0]
    n = b * s
    mlen = mem.shape[1]
    main_cols = 3 * SB_WIDTH + SSD_WIDTH + conv_w.shape[2]
    row = lambda v: v.astype(F32)[None, :]
    h = x.astype(F32).reshape(n, d)
    mem2 = mem.astype(F32).reshape(b * mlen, d)
    for i in range(depth):
        w_main = w_in[i, :, :main_cols].astype(BF16)
        w_dt = jnp.zeros((d, LANES), BF16).at[:, :SSD_HEADS].set(w_in[i, :, main_cols:].astype(BF16))
        proj, dt_raw = _norm_proj(h, row(norm_mix[i]), w_main, w_dt, tm=MATMUL_TILE)
        proj3 = proj.reshape(b, s, main_cols)
        attn = _sb_attention(proj3, blk=SB_BLOCK, heads_per_step=SB_HEADS_PER_STEP)
        ssd = _ssd(proj3, dt_raw.reshape(b, s, LANES), conv_w[i], conv_b[i], dt_bias[i], a_log[i],
                   d_skip[i], chunk=SSD_CHUNK)
        w_kv = jnp.concatenate([wk_x[i], wv_x[i]], axis=1).astype(BF16)
        kv = _norm_proj(mem2, row(mem_norm), w_kv, tm=MATMUL_TILE)
        h = _mix_cross(h.reshape(b, s, d), attn, ssd, row(sb_norm[i]), row(ssd_norm[i]),
                       w_out[i].astype(BF16), row(norm_cross[i]), wq_x[i].astype(BF16),
                       kv.reshape(b, mlen, 2 * d), wo_x[i].astype(BF16), tm=MATMUL_TILE).reshape(n, d)
        j = i // 2
        last = i == depth - 1
        if i % 2 == 0:
            h = _ffn(h, row(norm_ffn[i]), ffn_w1[j].astype(BF16), ffn_w3[j].astype(BF16),
                     ffn_w2[j].astype(BF16), tm=MATMUL_TILE)
            if last:
                h = _final_norm(h, row(final_norm), tm=ROW_TILE)
        else:
            h = _moe(h, row(norm_ffn[i]), router[j], moe_w1[j].astype(BF16), moe_w3[j].astype(BF16),
                     moe_w2[j].astype(BF16), row(final_norm), final_norm=last)
    return h.reshape(b, s, d)
```

```python
import functools

import jax
import jax.numpy as jnp
from jax import lax
from jax.experimental import pallas as pl
from jax.experimental.pallas import tpu as pltpu

F32 = jnp.float32
BF16 = jnp.bfloat16

EPS = 1e-6
HEAD_DIM = 64
SB_WIDTH = 512
SSD_WIDTH = 512
SSD_HEADS = 8
SSD_STATE = 128
SSD_GROUPS = 2
CONV_WIDTH = 4
X_HEADS = 4
N_EXPERTS = 8
LANES = 128
SUBLANES = 8
VMEM_LIMIT_BYTES = 56 * 1024 * 1024
UNDERFLOW_LOG = 105.0

SB_BLOCK = 256
SB_HEADS_PER_STEP = 8
SSD_CHUNK = 256
ROW_TILE = 512
MATMUL_TILE = 1024
MOE_TILE = 512
HIDDEN_CHUNK = 1024
DMA_UNROLL = 8


def _params(*sem):
    return pltpu.CompilerParams(dimension_semantics=sem, vmem_limit_bytes=VMEM_LIMIT_BYTES)


def _rms(x, g):
    return x * lax.rsqrt(jnp.mean(x * x, axis=-1, keepdims=True) + EPS) * g


def _silu(x):
    return x / (1.0 + jnp.exp(-x))


def _dot(a, b):
    return jnp.dot(a, b, preferred_element_type=F32)


def _dot_nt(a, b):
    return lax.dot_general(a, b, (((1,), (1,)), ((), ())), preferred_element_type=F32)


def _dot_tn(a, b):
    return lax.dot_general(a, b, (((0,), (0,)), ((), ())), preferred_element_type=F32)


def _split2(x):
    hi = x.astype(BF16)
    lo = (x - hi.astype(F32)).astype(BF16)
    return hi, lo


def _split3(x):
    hi = x.astype(BF16)
    r = x - hi.astype(F32)
    mid = r.astype(BF16)
    lo = (r - mid.astype(F32)).astype(BF16)
    return hi, mid, lo


def _norm_proj_kernel(x_ref, g_ref, w_ref, *rest, col_chunk, with_aux):
    if with_aux:
        waux_ref, o_ref, oaux_ref = rest
    else:
        (o_ref,) = rest
    u = _rms(x_ref[...], g_ref[...]).astype(BF16)
    for c in range(o_ref.shape[1] // col_chunk):
        sl = slice(c * col_chunk, (c + 1) * col_chunk)
        o_ref[:, sl] = _dot(u, w_ref[:, sl]).astype(o_ref.dtype)
    if with_aux:
        oaux_ref[...] = _dot(u, waux_ref[...])


def _norm_proj(x, g, w, w_aux=None, *, tm):
    n, d = x.shape
    m = w.shape[1]
    with_aux = w_aux is not None
    in_specs = [
        pl.BlockSpec((tm, d), lambda i: (i, 0)),
        pl.BlockSpec((1, d), lambda i: (0, 0)),
        _resident((d, m), lambda i: (0, 0)),
    ]
    out_shape = [jax.ShapeDtypeStruct((n, m), BF16)]
    out_specs = [pl.BlockSpec((tm, m), lambda i: (i, 0))]
    args = [x, g, w]
    if with_aux:
        in_specs.append(pl.BlockSpec((d, LANES), lambda i: (0, 0)))
        out_shape.append(jax.ShapeDtypeStruct((n, LANES), F32))
        out_specs.append(pl.BlockSpec((tm, LANES), lambda i: (i, 0)))
        args.append(w_aux)
    res = pl.pallas_call(
        functools.partial(_norm_proj_kernel, col_chunk=512, with_aux=with_aux),
        grid=(n // tm,),
        in_specs=in_specs,
        out_specs=out_specs,
        out_shape=out_shape,
        compiler_params=_params("parallel"),
        name="norm_proj_aux" if with_aux else "norm_proj",
    )(*args)
    return res if with_aux else res[0]


def _sb_kernel(q_ref, k_ref, v_ref, o_ref, carry_ref, acc_ref, *, blk, scale):
    nblk = q_ref.shape[0] // blk
    nheads = q_ref.shape[1] // HEAD_DIM
    rows = lax.broadcasted_iota(jnp.int32, (blk, blk), 0)
    cols = lax.broadcasted_iota(jnp.int32, (blk, blk), 1)
    suffix_ones = (rows >= cols).astype(BF16)
    before = cols < rows
    heads = [slice(hh * HEAD_DIM, (hh + 1) * HEAD_DIM) for hh in range(nheads)]

    def all_heads(qrows, kb, diag):
        ks = pl.ds(pl.multiple_of(kb * blk, blk), blk)
        zs = [_dot_nt(q_ref[qrows, lanes] * scale, k_ref[ks, lanes]) for lanes in heads]
        fails = [jnp.maximum(z, 0.0) + jnp.log(1.0 + jnp.exp(-jnp.abs(z))) for z in zs]
        if diag:
            fails = [jnp.where(before, f, 0.0) for f in fails]
        sufs = [_dot(f.astype(BF16), suffix_ones) for f in fails]
        ws = [jnp.exp(z - suf) for z, suf in zip(zs, sufs)]
        if diag:
            ws = [jnp.where(before, w, 0.0) for w in ws]
        pvs = [_dot(w.astype(BF16), v_ref[ks, lanes]) for w, lanes in zip(ws, heads)]
        low = None
        for hh in range(nheads):
            if diag:
                carry = sufs[hh][:, 0:1]
                acc_ref[hh] = pvs[hh]
            else:
                carry = carry_ref[hh]
                acc_ref[hh] += jnp.exp(-carry) * pvs[hh]
                carry = carry + sufs[hh][:, 0:1]
            carry_ref[hh] = carry
            low = jnp.min(carry) if low is None else jnp.minimum(low, jnp.min(carry))
        return low

    def q_block(i, _):
        qrows = pl.ds(pl.multiple_of(i * blk, blk), blk)
        low = all_heads(qrows, i, True)

        def live(st):
            return jnp.logical_and(st[0] >= 0, st[1] < UNDERFLOW_LOG)

        def body(st):
            return st[0] - 1, all_heads(qrows, st[0], False)

        lax.while_loop(live, body, (i - 1, low))
        o_ref[qrows, :] = jnp.concatenate([acc_ref[hh] for hh in range(nheads)], axis=-1)
        return 0

    lax.fori_loop(0, nblk, q_block, 0)


def _sb_attention(proj3, *, blk, heads_per_step):
    b, s, _ = proj3.shape
    width = heads_per_step * HEAD_DIM
    groups = SB_WIDTH // width
    spec = lambda off: pl.BlockSpec((None, s, width), lambda bi, hg: (bi, 0, off + hg))
    return pl.pallas_call(
        functools.partial(_sb_kernel, blk=blk, scale=HEAD_DIM ** -0.5),
        grid=(b, groups),
        in_specs=[spec(0), spec(groups), spec(2 * groups)],
        out_specs=pl.BlockSpec((None, s, width), lambda bi, hg: (bi, 0, hg)),
        out_shape=jax.ShapeDtypeStruct((b, s, SB_WIDTH), F32),
        scratch_shapes=[pltpu.VMEM((heads_per_step, blk, 1), F32),
                        pltpu.VMEM((heads_per_step, blk, HEAD_DIM), F32)],
        compiler_params=_params("parallel", "parallel"),
        name="stick_breaking",
    )(proj3, proj3, proj3)


def _ssd_kernel(xbc_ref, z_ref, dt_ref, cw_ref, cb_ref, dtb_ref, alog_ref, dskip_ref, expand_ref,
                o_ref, tail_ref, state_ref, *, chunk):
    c = pl.program_id(1)
    gn = SSD_GROUPS * SSD_STATE
    half = SSD_WIDTH // SSD_GROUPS

    @pl.when(c == 0)
    def _():
        tail_ref[...] = jnp.zeros(tail_ref.shape, F32)
        state_ref[...] = jnp.zeros(state_ref.shape, F32)

    rows = lax.broadcasted_iota(jnp.int32, (chunk, chunk), 0)
    cols = lax.broadcasted_iota(jnp.int32, (chunk, chunk), 1)
    causal = rows >= cols
    prefix_ones = causal.astype(BF16)

    x_b = xbc_ref[...]
    conv = cb_ref[...] + cw_ref[CONV_WIDTH - 1:CONV_WIDTH, :] * x_b.astype(F32)
    tail = tail_ref[...]
    head_row = lax.broadcasted_iota(jnp.int32, (SUBLANES, 1), 0)
    fix = jnp.zeros(tail.shape, F32)
    for back in range(1, CONV_WIDTH):
        w_tap = cw_ref[CONV_WIDTH - 1 - back:CONV_WIDTH - back, :]
        conv = conv + w_tap * _dot((rows - cols == back).astype(BF16), x_b)
        fix = fix + jnp.where(head_row < back, w_tap * pltpu.roll(tail, back, 0), 0.0)
    conv = jnp.concatenate([conv[:SUBLANES] + fix, conv[SUBLANES:]], axis=0)
    tail_ref[...] = x_b[chunk - SUBLANES:, :].astype(F32)
    act = _silu(conv)
    xs = act[:, :SSD_WIDTH]
    bm = act[:, SSD_WIDTH:SSD_WIDTH + gn].astype(BF16)
    cm = act[:, SSD_WIDTH + gn:].astype(BF16)

    raw = dt_ref[...] + dtb_ref[...]
    dt8 = jnp.maximum(raw, 0.0) + jnp.log(1.0 + jnp.exp(-jnp.abs(raw)))
    expand = expand_ref[...]
    l1, l2, l3 = _split3(dt8 * -jnp.exp(alog_ref[...]))
    la8 = _dot(prefix_ones, l1) + _dot(prefix_ones, l2) + _dot(prefix_ones, l3)
    d1, d2, d3 = _split3(dt8)
    dt = _dot(d1, expand) + _dot(d2, expand) + _dot(d3, expand)
    e1, e2, e3 = _split3(la8)
    la = _dot(e1, expand) + _dot(e2, expand) + _dot(e3, expand)
    la_t = jnp.transpose(la)
    xdt = xs * dt
    xdt_b = xdt.astype(BF16)

    y_parts = []
    for g in range(SSD_GROUPS):
        cb = _dot_nt(cm[:, g * SSD_STATE:(g + 1) * SSD_STATE], bm[:, g * SSD_STATE:(g + 1) * SSD_STATE])
        for r in range(SSD_HEADS // SSD_GROUPS):
            lane0 = (g * (SSD_HEADS // SSD_GROUPS) + r) * HEAD_DIM
            diff = la[:, lane0:lane0 + 1] - la_t[lane0:lane0 + 1, :]
            m = cb * jnp.exp(jnp.where(causal, diff, -jnp.inf))
            y_parts.append(_dot(m.astype(BF16), xdt_b[:, lane0:lane0 + HEAD_DIM]))
    y = jnp.concatenate(y_parts, axis=-1)

    state = state_ref[...]
    state_b = state.astype(BF16)
    y_off = jnp.concatenate(
        [_dot(cm[:, g * SSD_STATE:(g + 1) * SSD_STATE], state_b[:, g * half:(g + 1) * half])
         for g in range(SSD_GROUPS)], axis=-1)
    y = y + y_off * jnp.exp(la)
    la_last = la[chunk - 1:chunk, :]
    xdec = (xdt * jnp.exp(la_last - la)).astype(BF16)
    upd = jnp.concatenate(
        [_dot_tn(bm[:, g * SSD_STATE:(g + 1) * SSD_STATE], xdec[:, g * half:(g + 1) * half])
         for g in range(SSD_GROUPS)], axis=-1)
    state_ref[...] = state * jnp.exp(la_last) + upd

    y = y + xs * dskip_ref[...]
    o_ref[...] = y * _silu(z_ref[...].astype(F32))


def _ssd(proj3, dt3, conv_w, conv_b, dt_bias, a_log, d_skip, *, chunk):
    b, s, _ = proj3.shape
    conv_ch = conv_w.shape[1]
    rep = lambda v: jnp.repeat(v.astype(F32), HEAD_DIM)[None, :]
    pad = lambda v: jnp.zeros((1, LANES), F32).at[0, :SSD_HEADS].set(v.astype(F32))
    expand = (jnp.arange(LANES)[:, None] == (jnp.arange(SSD_WIDTH) // HEAD_DIM)[None, :]).astype(BF16)
    z_blk = (3 * SB_WIDTH) // SSD_WIDTH
    xbc_blk = (3 * SB_WIDTH + SSD_WIDTH) // conv_ch
    const = lambda shape: pl.BlockSpec(shape, lambda bi, ci: (0,) * len(shape))
    return pl.pallas_call(
        functools.partial(_ssd_kernel, chunk=chunk),
        grid=(b, s // chunk),
        in_specs=[
            pl.BlockSpec((None, chunk, conv_ch), lambda bi, ci: (bi, ci, xbc_blk)),
            pl.BlockSpec((None, chunk, SSD_WIDTH), lambda bi, ci: (bi, ci, z_blk)),
            pl.BlockSpec((None, chunk, LANES), lambda bi, ci: (bi, ci, 0)),
            const((CONV_WIDTH, conv_ch)), const((1, conv_ch)), const((1, LANES)),
            const((1, LANES)), const((1, SSD_WIDTH)), const((LANES, SSD_WIDTH)),
        ],
        out_specs=pl.BlockSpec((None, chunk, SSD_WIDTH), lambda bi, ci: (bi, ci, 0)),
        out_shape=jax.ShapeDtypeStruct((b, s, SSD_WIDTH), F32),
        scratch_shapes=[pltpu.VMEM((SUBLANES, conv_ch), F32),
                        pltpu.VMEM((SSD_STATE, SSD_WIDTH), F32)],
        compiler_params=_params("parallel", "arbitrary"),
        name="ssd",
    )(proj3, proj3, dt3, conv_w.astype(F32), conv_b.astype(F32)[None, :], pad(dt_bias),
      pad(a_log), rep(d_skip), expand)


def _mix_cross_kernel(h_ref, attn_ref, ssd_ref, sbg_ref, ssdg_ref, wout_ref, ng_ref, wq_ref,
                      kv_ref, wo_ref, o_ref, *, scale):
    d = h_ref.shape[1]
    hd = d // X_HEADS
    a = _rms(attn_ref[...], sbg_ref[...]).astype(BF16)
    s = _rms(ssd_ref[...], ssdg_ref[...]).astype(BF16)
    h = h_ref[...] + _dot(a, wout_ref[0:SB_WIDTH, :]) + _dot(s, wout_ref[SB_WIDTH:, :])
    u = _rms(h, ng_ref[...]).astype(BF16)
    q = _dot(u, wq_ref[...]).astype(BF16)
    heads = []
    for hh in range(X_HEADS):
        lanes = slice(hh * hd, (hh + 1) * hd)
        sc = _dot_nt(q[:, lanes], kv_ref[:, lanes]) * scale
        p = jnp.exp(sc - jnp.max(sc, axis=-1, keepdims=True))
        p = p / jnp.sum(p, axis=-1, keepdims=True)
        heads.append(_dot(p.astype(BF16), kv_ref[:, d + hh * hd:d + (hh + 1) * hd]).astype(BF16))
    o_ref[...] = h + _dot(jnp.concatenate(heads, axis=-1), wo_ref[...])


def _mix_cross(h3, attn3, ssd3, sb_g, ssd_g, w_out, norm_g, wq, kv3, wo, *, tm):
    b, s, d = h3.shape
    mem = kv3.shape[1]
    tm = min(tm, s)
    const = lambda shape: _resident(shape, lambda bi, ti: (0,) * len(shape))
    tile = lambda w: pl.BlockSpec((None, tm, w), lambda bi, ti: (bi, ti, 0))
    return pl.pallas_call(
        functools.partial(_mix_cross_kernel, scale=(d // X_HEADS) ** -0.5),
        grid=(b, s // tm),
        in_specs=[tile(d), tile(SB_WIDTH), tile(SSD_WIDTH), const((1, SB_WIDTH)), const((1, SSD_WIDTH)),
                  const((SB_WIDTH + SSD_WIDTH, d)), const((1, d)), const((d, d)),
                  pl.BlockSpec((None, mem, 2 * d), lambda bi, ti: (bi, 0, 0)), const((d, d))],
        out_specs=tile(d),
        out_shape=jax.ShapeDtypeStruct((b, s, d), F32),
        compiler_params=_params("parallel", "parallel"),
        name="mix_cross",
    )(h3, attn3, ssd3, sb_g, ssd_g, w_out, norm_g, wq, kv3, wo)


def _swiglu_hidden(x, w1_ref, w3_ref, act_ref):
    ff = w1_ref.shape[1]
    for c0 in range(0, ff, HIDDEN_CHUNK):
        sl = slice(c0, min(c0 + HIDDEN_CHUNK, ff))
        act_ref[:, sl] = (_silu(_dot(x, w1_ref[:, sl])) * _dot(x, w3_ref[:, sl])).astype(BF16)


def _ffn_kernel(h_ref, g_ref, w1_ref, w3_ref, w2_ref, o_ref, act_ref):
    h = h_ref[...]
    _swiglu_hidden(_rms(h, g_ref[...]).astype(BF16), w1_ref, w3_ref, act_ref)
    o_ref[...] = h + _dot(act_ref[...], w2_ref[...])


def _resident(shape, index_map):
    return pl.BlockSpec(shape, index_map, pipeline_mode=pl.Buffered(1))


def _ffn(h, g, w1, w3, w2, *, layer, tm):
    n, d = h.shape
    ff = w1.shape[2]
    return pl.pallas_call(
        _ffn_kernel,
        grid=(n // tm,),
        in_specs=[pl.BlockSpec((tm, d), lambda i: (i, 0)),
                  pl.BlockSpec((1, d), lambda i: (0, 0)),
                  _resident((None, d, ff), lambda i: (layer, 0, 0)),
                  _resident((None, d, ff), lambda i: (layer, 0, 0)),
                  _resident((None, ff, d), lambda i: (layer, 0, 0))],
        out_specs=pl.BlockSpec((tm, d), lambda i: (i, 0)),
        out_shape=jax.ShapeDtypeStruct((n, d), F32),
        scratch_shapes=[pltpu.VMEM((tm, ff), BF16)],
        compiler_params=_params("parallel"),
        name="ffn",
    )(h, g, w1, w3, w2)


def _store_row_tiles(ref, x):
    parts = x.shape[1] // LANES
    for j in range(parts):
        ref[pl.ds(j, x.shape[0], stride=parts), :] = x[:, j * LANES:(j + 1) * LANES]


def _load_row_tiles(ref, rows, parts):
    return jnp.concatenate([ref[pl.ds(j, rows, stride=parts), :] for j in range(parts)], axis=-1)


def _router_kernel(h_ref, g_ref, wr_ref, u_ref, meta_ref, cnt_ref, run_ref):
    i = pl.program_id(0)
    tm = h_ref.shape[0]

    @pl.when(i == 0)
    def _():
        run_ref[...] = jnp.zeros(run_ref.shape, F32)

    u = _rms(h_ref[...], g_ref[...])
    _store_row_tiles(u_ref, u)
    uh, ul = _split2(u)
    part = _dot(uh, wr_ref[...])
    logits = part[:, :LANES] + part[:, LANES:] + _dot(ul, wr_ref[:, :LANES])
    lane = lax.broadcasted_iota(jnp.int32, (tm, LANES), 1)
    logits = jnp.where(lane < N_EXPERTS, logits, -jnp.inf)
    m1 = jnp.max(logits, axis=-1, keepdims=True)
    e1 = jnp.min(jnp.where(logits == m1, lane, LANES), axis=-1, keepdims=True)
    rest = jnp.where(lane == e1, -jnp.inf, logits)
    m2 = jnp.max(rest, axis=-1, keepdims=True)
    e2 = jnp.min(jnp.where(rest == m2, lane, LANES), axis=-1, keepdims=True)
    t = jnp.exp(m2 - m1)
    g1 = 1.0 / (1.0 + t)
    g2 = t / (1.0 + t)
    oh1 = (lane == e1).astype(F32)
    oh2 = (lane == e2).astype(F32)
    both = oh1 + oh2
    rows = lax.broadcasted_iota(jnp.int32, (tm, tm), 0)
    cols = lax.broadcasted_iota(jnp.int32, (tm, tm), 1)
    earlier = (cols < rows).astype(BF16)
    before = _dot(earlier, both.astype(BF16)) + run_ref[...]
    r1 = jnp.sum(before * oh1, axis=-1, keepdims=True)
    r2 = jnp.sum(before * oh2, axis=-1, keepdims=True)
    meta = jnp.where(lane == 0, e1.astype(F32), 0.0)
    meta = jnp.where(lane == 1, e2.astype(F32), meta)
    meta = jnp.where(lane == 2, g1, meta)
    meta = jnp.where(lane == 3, g2, meta)
    meta = jnp.where(lane == 4, r1, meta)
    meta = jnp.where(lane == 5, r2, meta)
    meta_ref[...] = meta
    run_ref[...] += jnp.sum(both, axis=0, keepdims=True)
    cnt_ref[...] = run_ref[...]


def _router(h, g, wr, *, tm):
    n, d = h.shape
    parts = d // LANES
    return pl.pallas_call(
        _router_kernel,
        grid=(n // tm,),
        in_specs=[pl.BlockSpec((tm, d), lambda i: (i, 0)),
                  pl.BlockSpec((1, d), lambda i: (0, 0)),
                  pl.BlockSpec((d, 2 * LANES), lambda i: (0, 0))],
        out_specs=[pl.BlockSpec((tm * parts, LANES), lambda i: (i, 0)),
                   pl.BlockSpec((tm, LANES), lambda i: (i, 0)),
                   pl.BlockSpec((1, LANES), lambda i: (0, 0))],
        out_shape=[jax.ShapeDtypeStruct((n * parts, LANES), F32),
                   jax.ShapeDtypeStruct((n, LANES), F32),
                   jax.ShapeDtypeStruct((1, LANES), F32)],
        scratch_shapes=[pltpu.VMEM((1, LANES), F32)],
        compiler_params=_params("arbitrary"),
        name="router",
    )(h, g, wr)


def _tile_rows(first_row, nrows, parts):
    return pl.ds(pl.multiple_of(first_row * parts, parts), nrows * parts)


def _scatter_kernel(dest_ref, fill_ref, u_ref, xs_ref, zero_ref, sems, *, parts):
    tm = u_ref.shape[0] // parts

    @pl.when(pl.program_id(0) == 0)
    def _():
        zero_ref[...] = jnp.zeros(zero_ref.shape, zero_ref.dtype)
        for e in range(N_EXPERTS):
            fill = pltpu.make_async_copy(zero_ref, xs_ref.at[_tile_rows(fill_ref[e], MOE_TILE, parts)],
                                         sems.at[2])
            fill.start()
            fill.wait()

    def start(r, _):
        for k in range(2):
            pltpu.make_async_copy(u_ref.at[_tile_rows(r, 1, parts)],
                                  xs_ref.at[_tile_rows(dest_ref[0, 2 * r + k], 1, parts)],
                                  sems.at[k]).start(priority=k)
        return 0

    lax.fori_loop(0, tm, start, 0, unroll=DMA_UNROLL)
    for k in range(2):
        pltpu.make_async_copy(u_ref, xs_ref.at[_tile_rows(0, tm, parts)], sems.at[k]).wait()


def _dest_spec(tm):
    return pl.BlockSpec((None, 1, 2 * tm), lambda i: (i, 0, 0), memory_space=pltpu.SMEM)


def _scatter_rows(dest, fill_start, u, cap, *, tm, parts):
    return pl.pallas_call(
        functools.partial(_scatter_kernel, parts=parts),
        grid=(u.shape[0] // (tm * parts),),
        in_specs=[_dest_spec(tm),
                  pl.BlockSpec(memory_space=pltpu.SMEM),
                  pl.BlockSpec((tm * parts, LANES), lambda i: (i, 0))],
        out_specs=pl.BlockSpec(memory_space=pl.ANY),
        out_shape=jax.ShapeDtypeStruct((cap * parts, LANES), u.dtype),
        scratch_shapes=[pltpu.VMEM((MOE_TILE * parts, LANES), u.dtype), pltpu.SemaphoreType.DMA((3,))],
        compiler_params=_params("arbitrary"),
        name="moe_scatter",
    )(dest, fill_start, u)


def _expert_kernel(be_ref, used_ref, x_ref, w1_ref, w3_ref, w2_ref, o_ref, act_ref):
    del be_ref
    i = pl.program_id(0)
    tm = act_ref.shape[0]
    parts = x_ref.shape[0] // tm

    @pl.when(i < used_ref[0])
    def _():
        _swiglu_hidden(_load_row_tiles(x_ref, tm, parts).astype(BF16), w1_ref, w3_ref, act_ref)
        _store_row_tiles(o_ref, _dot(act_ref[...], w2_ref[...]))

    @pl.when(i >= used_ref[0])
    def _():
        o_ref[...] = jnp.zeros(o_ref.shape, F32)


def _experts(block_e, used, xs, w1, w3, w2, *, layer, tm):
    d, ff = w1.shape[2], w1.shape[3]
    parts = d // LANES
    blocks = xs.shape[0] // (tm * parts)
    return pl.pallas_call(
        _expert_kernel,
        grid_spec=pltpu.PrefetchScalarGridSpec(
            num_scalar_prefetch=2,
            grid=(blocks,),
            in_specs=[pl.BlockSpec((tm * parts, LANES), lambda i, be, nu: (jnp.minimum(i, nu[0] - 1), 0)),
                      _resident((None, None, d, ff), lambda i, be, nu: (layer, be[i], 0, 0)),
                      _resident((None, None, d, ff), lambda i, be, nu: (layer, be[i], 0, 0)),
                      _resident((None, None, ff, d), lambda i, be, nu: (layer, be[i], 0, 0))],
            out_specs=pl.BlockSpec((tm * parts, LANES), lambda i, be, nu: (i, 0)),
            scratch_shapes=[pltpu.VMEM((tm, ff), BF16)],
        ),
        out_shape=jax.ShapeDtypeStruct(xs.shape, F32),
        compiler_params=_params("arbitrary"),
        name="moe_experts",
    )(block_e, used, xs, w1, w3, w2)


def _combine_kernel(dest_ref, next_ref, h_ref, meta_ref, y_ref, fg_ref, o_ref, buf_ref, sems, *,
                    final_norm):
    tm, d = h_ref.shape
    parts = d // LANES
    i = pl.program_id(0)

    def gather(idx_ref, phase):
        def start(r, _):
            for k in range(2):
                pltpu.make_async_copy(y_ref.at[_tile_rows(idx_ref[0, 2 * r + k], 1, parts)],
                                      buf_ref.at[phase, k, _tile_rows(r, 1, parts)],
                                      sems.at[phase, k]).start(priority=k)
            return 0

        lax.fori_loop(0, tm, start, 0, unroll=DMA_UNROLL)

    @pl.when(i == 0)
    def _():
        gather(dest_ref, 0)

    @pl.when(i + 1 < pl.num_programs(0))
    def _():
        gather(next_ref, (i + 1) % 2)

    phase = i % 2
    for k in range(2):
        pltpu.make_async_copy(y_ref.at[_tile_rows(0, tm, parts)], buf_ref.at[phase, k],
                              sems.at[phase, k]).wait()
    meta = meta_ref[...]
    out = (h_ref[...] + meta[:, 2:3] * _load_row_tiles(buf_ref.at[phase, 0], tm, parts)
           + meta[:, 3:4] * _load_row_tiles(buf_ref.at[phase, 1], tm, parts))
    if final_norm:
        out = _rms(out, fg_ref[...])
    o_ref[...] = out


def _combine(dest, h, meta, y, fg, *, tm, final_norm):
    n, d = h.shape
    last = n // tm - 1
    next_spec = pl.BlockSpec((None, 1, 2 * tm), lambda i: (jnp.minimum(i + 1, last), 0, 0),
                             memory_space=pltpu.SMEM)
    return pl.pallas_call(
        functools.partial(_combine_kernel, final_norm=final_norm),
        grid=(n // tm,),
        in_specs=[_dest_spec(tm), next_spec,
                  pl.BlockSpec((tm, d), lambda i: (i, 0)),
                  pl.BlockSpec((tm, LANES), lambda i: (i, 0)),
                  pl.BlockSpec(memory_space=pl.ANY),
                  pl.BlockSpec((1, d), lambda i: (0, 0))],
        out_specs=pl.BlockSpec((tm, d), lambda i: (i, 0)),
        out_shape=jax.ShapeDtypeStruct((n, d), F32),
        scratch_shapes=[pltpu.VMEM((2, 2, tm * (d // LANES), LANES), F32),
                        pltpu.SemaphoreType.DMA((2, 2))],
        compiler_params=_params("arbitrary"),
        name="moe_combine",
    )(dest, dest, h, meta, y, fg)


def _final_norm_kernel(h_ref, g_ref, o_ref):
    o_ref[...] = _rms(h_ref[...], g_ref[...])


def _final_norm(h, g, *, tm):
    n, d = h.shape
    return pl.pallas_call(
        _final_norm_kernel,
        grid=(n // tm,),
        in_specs=[pl.BlockSpec((tm, d), lambda i: (i, 0)), pl.BlockSpec((1, d), lambda i: (0, 0))],
        out_specs=pl.BlockSpec((tm, d), lambda i: (i, 0)),
        out_shape=jax.ShapeDtypeStruct((n, d), F32),
        compiler_params=_params("parallel"),
        name="final_norm",
    )(h, g)


def _moe(h, g, router_w, w1, w3, w2, final_g, *, layer, final_norm):
    n, d = h.shape
    assert (d // LANES) % SUBLANES == 0, "row-tile layout needs whole (8, 128) tiles per token row"
    tm = MOE_TILE
    wr = jnp.zeros((d, LANES), F32).at[:, :N_EXPERTS].set(router_w.astype(F32))
    u, meta, counts = _router(h, g, jnp.concatenate(_split2(wr), axis=1), tm=ROW_TILE)
    counts = counts[0, :N_EXPERTS].astype(jnp.int32)
    padded = (counts + tm - 1) // tm * tm
    pends = jnp.cumsum(padded)
    pstarts = pends - padded
    experts = meta[:, 0:2].astype(jnp.int32)
    dest = (pstarts[experts] + meta[:, 4:6].astype(jnp.int32)).reshape(n // ROW_TILE, 1, 2 * ROW_TILE)
    cap = (-(-2 * n // tm)) * tm + N_EXPERTS * tm
    block_start = jnp.arange(cap // tm, dtype=jnp.int32) * tm
    block_e = jnp.minimum(jnp.sum((block_start[:, None] >= pends[None, :]).astype(jnp.int32), axis=1),
                          N_EXPERTS - 1)
    used = (pends[-1:] // tm).astype(jnp.int32)
    xs = _scatter_rows(dest, jnp.maximum(pends - tm, 0), u, cap, tm=ROW_TILE, parts=d // LANES)
    y = _experts(block_e, used, xs, w1, w3, w2, layer=layer, tm=tm)
    return _combine(dest, h, meta, y, final_g, tm=ROW_TILE, final_norm=final_norm)


def kernel(x, mem, mem_norm, norm_mix, w_in, conv_w, conv_b, dt_bias, a_log, d_skip, sb_norm, ssd_norm, w_out, norm_cross, wq_x, wk_x, wv_x, wo_x, norm_ffn, ffn_w1, ffn_w3, ffn_w2, router, moe_w1, moe_w3, moe_w2, final_norm):
    b, s, d = x.shape
    depth = w_in.shape[0]
    n = b * s
    mlen = mem.shape[1]
    main_cols = 3 * SB_WIDTH + SSD_WIDTH + conv_w.shape[2]
    row = lambda v: v.astype(F32)[None, :]
    h = x.astype(F32).reshape(n, d)
    mem2 = mem.astype(F32).reshape(b * mlen, d)
    dense_w = [w.astype(BF16) for w in (ffn_w1, ffn_w3, ffn_w2)]
    expert_w = [w.astype(BF16) for w in (moe_w1, moe_w3, moe_w2)]
    for i in range(depth):
        w_main = w_in[i, :, :main_cols].astype(BF16)
        w_dt = jnp.zeros((d, LANES), BF16).at[:, :SSD_HEADS].set(w_in[i, :, main_cols:].astype(BF16))
        proj, dt_raw = _norm_proj(h, row(norm_mix[i]), w_main, w_dt, tm=MATMUL_TILE)
        proj3 = proj.reshape(b, s, main_cols)
        attn = _sb_attention(proj3, blk=SB_BLOCK, heads_per_step=SB_HEADS_PER_STEP)
        ssd = _ssd(proj3, dt_raw.reshape(b, s, LANES), conv_w[i], conv_b[i], dt_bias[i], a_log[i],
                   d_skip[i], chunk=SSD_CHUNK)
        w_kv = jnp.concatenate([wk_x[i], wv_x[i]], axis=1).astype(BF16)
        kv = _norm_proj(mem2, row(mem_norm), w_kv, tm=MATMUL_TILE)
        h = _mix_cross(h.reshape(b, s, d), attn, ssd, row(sb_norm[i]), row(ssd_norm[i]),
                       w_out[i].astype(BF16), row(norm_cross[i]), wq_x[i].astype(BF16),
                       kv.reshape(b, mlen, 2 * d), wo_x[i].astype(BF16), tm=MATMUL_TILE).reshape(n, d)
        j = i // 2
        last = i == depth - 1
        if i % 2 == 0:
            h = _ffn(h, row(norm_ffn[i]), *dense_w, layer=j, tm=MATMUL_TILE)
            if last:
                h = _final_norm(h, row(final_norm), tm=ROW_TILE)
        else:
            h = _moe(h, row(norm_ffn[i]), router[j], *expert_w, row(final_norm), layer=j, final_norm=last)
    return h.reshape(b, s, d)
```

```python
import functools

import jax
import jax.numpy as jnp
from jax import lax
from jax.experimental import pallas as pl
from jax.experimental.pallas import tpu as pltpu

F32 = jnp.float32
BF16 = jnp.bfloat16

EPS = 1e-6
HEAD_DIM = 64
SB_WIDTH = 512
SSD_WIDTH = 512
SSD_HEADS = 8
SSD_STATE = 128
SSD_GROUPS = 2
CONV_WIDTH = 4
X_HEADS = 4
N_EXPERTS = 8
LANES = 128
SUBLANES = 8
VMEM_LIMIT_BYTES = 56 * 1024 * 1024
UNDERFLOW_LOG = 105.0

SB_BLOCK = 256
SB_HEADS_PER_STEP = 8
SSD_CHUNK = 256
SSD_ROWS_PER_STEP = 2
ROW_TILE = 512
MATMUL_TILE = 1024
MOE_TILE = 512
HIDDEN_CHUNK = 1024
DMA_UNROLL = 8


def _params(*sem):
    return pltpu.CompilerParams(dimension_semantics=sem, vmem_limit_bytes=VMEM_LIMIT_BYTES)


def _rms(x, g):
    return x * lax.rsqrt(jnp.mean(x * x, axis=-1, keepdims=True) + EPS) * g


def _silu(x):
    return x / (1.0 + jnp.exp(-x))


def _dot(a, b):
    return jnp.dot(a, b, preferred_element_type=F32)


def _dot_nt(a, b):
    return lax.dot_general(a, b, (((1,), (1,)), ((), ())), preferred_element_type=F32)


def _dot_tn(a, b):
    return lax.dot_general(a, b, (((0,), (0,)), ((), ())), preferred_element_type=F32)


def _split2(x):
    hi = x.astype(BF16)
    lo = (x - hi.astype(F32)).astype(BF16)
    return hi, lo


def _split3(x):
    hi = x.astype(BF16)
    r = x - hi.astype(F32)
    mid = r.astype(BF16)
    lo = (r - mid.astype(F32)).astype(BF16)
    return hi, mid, lo


def _norm_proj_kernel(x_ref, g_ref, w_ref, *rest, col_chunk, with_aux):
    if with_aux:
        waux_ref, o_ref, oaux_ref = rest
    else:
        (o_ref,) = rest
    u = _rms(x_ref[...], g_ref[...]).astype(BF16)
    for c in range(o_ref.shape[1] // col_chunk):
        sl = slice(c * col_chunk, (c + 1) * col_chunk)
        o_ref[:, sl] = _dot(u, w_ref[:, sl]).astype(o_ref.dtype)
    if with_aux:
        oaux_ref[...] = _dot(u, waux_ref[...])


def _norm_proj(x, g, w, w_aux=None, *, tm):
    n, d = x.shape
    m = w.shape[1]
    with_aux = w_aux is not None
    in_specs = [
        pl.BlockSpec((tm, d), lambda i: (i, 0)),
        pl.BlockSpec((1, d), lambda i: (0, 0)),
        _resident((d, m), lambda i: (0, 0)),
    ]
    out_shape = [jax.ShapeDtypeStruct((n, m), BF16)]
    out_specs = [pl.BlockSpec((tm, m), lambda i: (i, 0))]
    args = [x, g, w]
    if with_aux:
        in_specs.append(pl.BlockSpec((d, LANES), lambda i: (0, 0)))
        out_shape.append(jax.ShapeDtypeStruct((n, LANES), F32))
        out_specs.append(pl.BlockSpec((tm, LANES), lambda i: (i, 0)))
        args.append(w_aux)
    res = pl.pallas_call(
        functools.partial(_norm_proj_kernel, col_chunk=512, with_aux=with_aux),
        grid=(n // tm,),
        in_specs=in_specs,
        out_specs=out_specs,
        out_shape=out_shape,
        compiler_params=_params("parallel"),
        name="norm_proj_aux" if with_aux else "norm_proj",
    )(*args)
    return res if with_aux else res[0]


def _sb_kernel(q_ref, k_ref, v_ref, o_ref, carry_ref, acc_ref, *, blk, scale):
    nblk = q_ref.shape[0] // blk
    nheads = q_ref.shape[1] // HEAD_DIM
    rows = lax.broadcasted_iota(jnp.int32, (blk, blk), 0)
    cols = lax.broadcasted_iota(jnp.int32, (blk, blk), 1)
    suffix_ones = (rows >= cols).astype(BF16)
    before = cols < rows
    heads = [slice(hh * HEAD_DIM, (hh + 1) * HEAD_DIM) for hh in range(nheads)]

    def sweep(tile0, parts):
        chains = [(part, hh) for part in parts for hh in range(nheads)]
        qrows = lambda p: pl.ds(pl.multiple_of(tile0 + p[0], p[1]), p[1])
        krows = lambda p: pl.ds(pl.multiple_of(p[2], p[3]), p[3])
        local = lambda p: slice(p[0], p[0] + p[1])
        mask = lambda p: before[local(p), :p[3]]
        zs = [_dot_nt(q_ref[qrows(p), heads[hh]] * scale, k_ref[krows(p), heads[hh]]) for p, hh in chains]
        fails = [jnp.maximum(z, 0.0) + jnp.log(1.0 + jnp.exp(-jnp.abs(z))) for z in zs]
        fails = [jnp.where(mask(p), f, 0.0) if p[4] == "diag" else f for f, (p, _) in zip(fails, chains)]
        sufs = [_dot(f.astype(BF16), suffix_ones[:p[3], :p[3]]) for f, (p, _) in zip(fails, chains)]
        ws = [jnp.exp(z - suf) for z, suf in zip(zs, sufs)]
        ws = [jnp.where(mask(p), w, 0.0) if p[4] == "diag" else w for w, (p, _) in zip(ws, chains)]
        pvs = [_dot(w.astype(BF16), v_ref[krows(p), heads[hh]]) for w, (p, hh) in zip(ws, chains)]
        carries, accs = {}, {}
        for (p, hh), suf, pv in zip(chains, sufs, pvs):
            key = (p[0], hh)
            if p[4] == "diag":
                carries[key], accs[key] = suf[:, 0:1], pv
            else:
                carry = carries[key] if p[4] == "prev" else carry_ref[hh, local(p), :]
                acc = accs[key] if p[4] == "prev" else acc_ref[hh, local(p), :]
                carries[key], accs[key] = carry + suf[:, 0:1], acc + jnp.exp(-carry) * pv
        low = None
        for (p, hh) in chains:
            if p[4] == "prev":
                continue
            carry = carries[(p[0], hh)]
            carry_ref[hh, local(p), :] = carry
            acc_ref[hh, local(p), :] = accs[(p[0], hh)]
            low = jnp.min(carry) if low is None else jnp.minimum(low, jnp.min(carry))
        return low

    half = blk // 2

    def q_block(i, _):
        tile0 = i * blk
        qrows = pl.ds(pl.multiple_of(tile0, blk), blk)
        diag = [(0, half, tile0, half, "diag"), (half, half, tile0, blk, "diag")]
        prev = [(0, half, tile0 - blk, blk, "prev"), (half, half, tile0 - blk, blk, "prev")]
        low = lax.cond(i == 0, lambda: sweep(tile0, diag), lambda: sweep(tile0, diag + prev))

        def live(st):
            return jnp.logical_and(st[0] >= 0, st[1] < UNDERFLOW_LOG)

        def body(st):
            return st[0] - 1, sweep(tile0, [(0, blk, st[0] * blk, blk, "loop")])

        lax.while_loop(live, body, (i - 2, low))
        o_ref[qrows, :] = jnp.concatenate([acc_ref[hh] for hh in range(nheads)], axis=-1)
        return 0

    lax.fori_loop(0, nblk, q_block, 0)


def _sb_attention(proj3, *, blk, heads_per_step):
    b, s, _ = proj3.shape
    width = heads_per_step * HEAD_DIM
    groups = SB_WIDTH // width
    spec = lambda off: pl.BlockSpec((None, s, width), lambda bi, hg: (bi, 0, off + hg))
    return pl.pallas_call(
        functools.partial(_sb_kernel, blk=blk, scale=HEAD_DIM ** -0.5),
        grid=(b, groups),
        in_specs=[spec(0), spec(groups), spec(2 * groups)],
        out_specs=pl.BlockSpec((None, s, width), lambda bi, hg: (bi, 0, hg)),
        out_shape=jax.ShapeDtypeStruct((b, s, SB_WIDTH), F32),
        scratch_shapes=[pltpu.VMEM((heads_per_step, blk, 1), F32),
                        pltpu.VMEM((heads_per_step, blk, HEAD_DIM), F32)],
        compiler_params=_params("parallel", "parallel"),
        name="stick_breaking",
    )(proj3, proj3, proj3)


def _ssd_kernel(xbc_ref, z_ref, dt_ref, cw_ref, cb_ref, dtb_ref, alog_ref, dskip_ref, expand_ref,
                o_ref, tail_ref, state_ref, *, chunk):
    c = pl.program_id(1)
    gn = SSD_GROUPS * SSD_STATE
    half = SSD_WIDTH // SSD_GROUPS

    @pl.when(c == 0)
    def _():
        tail_ref[...] = jnp.zeros(tail_ref.shape, F32)
        state_ref[...] = jnp.zeros(state_ref.shape, F32)

    rows = lax.broadcasted_iota(jnp.int32, (chunk, chunk), 0)
    cols = lax.broadcasted_iota(jnp.int32, (chunk, chunk), 1)
    causal = rows >= cols
    prefix_ones = causal.astype(BF16)

    nrows = xbc_ref.shape[0]
    each = range(nrows)
    grp = lambda g: slice(g * SSD_STATE, (g + 1) * SSD_STATE)

    x_b = [xbc_ref[r] for r in each]
    tails = [tail_ref[r] for r in each]
    convs = [cb_ref[...] + cw_ref[CONV_WIDTH - 1:CONV_WIDTH, :] * x_b[r].astype(F32) for r in each]
    head_row = lax.broadcasted_iota(jnp.int32, (SUBLANES, 1), 0)
    fixes = [jnp.zeros(tails[r].shape, F32) for r in each]
    for back in range(1, CONV_WIDTH):
        w_tap = cw_ref[CONV_WIDTH - 1 - back:CONV_WIDTH - back, :]
        shift = (rows - cols == back).astype(BF16)
        convs = [convs[r] + w_tap * _dot(shift, x_b[r]) for r in each]
        fixes = [fixes[r] + jnp.where(head_row < back, w_tap * pltpu.roll(tails[r], back, 0), 0.0)
                 for r in each]
    for r in each:
        tail_ref[r] = x_b[r][chunk - SUBLANES:, :].astype(F32)
    acts = [_silu(jnp.concatenate([convs[r][:SUBLANES] + fixes[r], convs[r][SUBLANES:]], axis=0))
            for r in each]
    xs = [a[:, :SSD_WIDTH] for a in acts]
    bm = [a[:, SSD_WIDTH:SSD_WIDTH + gn].astype(BF16) for a in acts]
    cm = [a[:, SSD_WIDTH + gn:].astype(BF16) for a in acts]

    expand = expand_ref[...]
    sum3 = lambda pieces, f: f(pieces[0]) + f(pieces[1]) + f(pieces[2])
    raws = [dt_ref[r] + dtb_ref[...] for r in each]
    dt8 = [jnp.maximum(x, 0.0) + jnp.log(1.0 + jnp.exp(-jnp.abs(x))) for x in raws]
    neg_a = -jnp.exp(alog_ref[...])
    la8 = [sum3(_split3(d * neg_a), lambda p: _dot(prefix_ones, p)) for d in dt8]
    dts = [sum3(_split3(d), lambda p: _dot(p, expand)) for d in dt8]
    las = [sum3(_split3(l), lambda p: _dot(p, expand)) for l in la8]
    la_ts = [jnp.transpose(l) for l in las]
    xdts = [xs[r] * dts[r] for r in each]
    xdt_b = [x.astype(BF16) for x in xdts]

    y_parts = [[] for _ in each]
    for g in range(SSD_GROUPS):
        cbs = [_dot_nt(cm[r][:, grp(g)], bm[r][:, grp(g)]) for r in each]
        for hh in range(SSD_HEADS // SSD_GROUPS):
            lane0 = (g * (SSD_HEADS // SSD_GROUPS) + hh) * HEAD_DIM
            for r in each:
                diff = las[r][:, lane0:lane0 + 1] - la_ts[r][lane0:lane0 + 1, :]
                m = cbs[r] * jnp.exp(jnp.where(causal, diff, -jnp.inf))
                y_parts[r].append(_dot(m.astype(BF16), xdt_b[r][:, lane0:lane0 + HEAD_DIM]))

    for r in each:
        state = state_ref[r]
        state_b = state.astype(BF16)
        y_off = jnp.concatenate(
            [_dot(cm[r][:, grp(g)], state_b[:, g * half:(g + 1) * half]) for g in range(SSD_GROUPS)],
            axis=-1)
        y = jnp.concatenate(y_parts[r], axis=-1) + y_off * jnp.exp(las[r])
        la_last = las[r][chunk - 1:chunk, :]
        xdec = (xdts[r] * jnp.exp(la_last - las[r])).astype(BF16)
        upd = jnp.concatenate(
            [_dot_tn(bm[r][:, grp(g)], xdec[:, g * half:(g + 1) * half]) for g in range(SSD_GROUPS)],
            axis=-1)
        state_ref[r] = state * jnp.exp(la_last) + upd
        y = y + xs[r] * dskip_ref[...]
        o_ref[r] = y * _silu(z_ref[r].astype(F32))


def _ssd(proj3, dt3, conv_w, conv_b, dt_bias, a_log, d_skip, *, chunk):
    b, s, _ = proj3.shape
    conv_ch = conv_w.shape[1]
    rep = lambda v: jnp.repeat(v.astype(F32), HEAD_DIM)[None, :]
    pad = lambda v: jnp.zeros((1, LANES), F32).at[0, :SSD_HEADS].set(v.astype(F32))
    expand = (jnp.arange(LANES)[:, None] == (jnp.arange(SSD_WIDTH) // HEAD_DIM)[None, :]).astype(BF16)
    z_blk = (3 * SB_WIDTH) // SSD_WIDTH
    xbc_blk = (3 * SB_WIDTH + SSD_WIDTH) // conv_ch
    const = lambda shape: pl.BlockSpec(shape, lambda bi, ci: (0,) * len(shape))
    nb = SSD_ROWS_PER_STEP
    return pl.pallas_call(
        functools.partial(_ssd_kernel, chunk=chunk),
        grid=(b // nb, s // chunk),
        in_specs=[
            pl.BlockSpec((nb, chunk, conv_ch), lambda bi, ci: (bi, ci, xbc_blk)),
            pl.BlockSpec((nb, chunk, SSD_WIDTH), lambda bi, ci: (bi, ci, z_blk)),
            pl.BlockSpec((nb, chunk, LANES), lambda bi, ci: (bi, ci, 0)),
            const((CONV_WIDTH, conv_ch)), const((1, conv_ch)), const((1, LANES)),
            const((1, LANES)), const((1, SSD_WIDTH)), const((LANES, SSD_WIDTH)),
        ],
        out_specs=pl.BlockSpec((nb, chunk, SSD_WIDTH), lambda bi, ci: (bi, ci, 0)),
        out_shape=jax.ShapeDtypeStruct((b, s, SSD_WIDTH), F32),
        scratch_shapes=[pltpu.VMEM((nb, SUBLANES, conv_ch), F32),
                        pltpu.VMEM((nb, SSD_STATE, SSD_WIDTH), F32)],
        compiler_params=_params("parallel", "arbitrary"),
        name="ssd",
    )(proj3, proj3, dt3, conv_w.astype(F32), conv_b.astype(F32)[None, :], pad(dt_bias),
      pad(a_log), rep(d_skip), expand)


def _mix_cross_kernel(h_ref, attn_ref, ssd_ref, sbg_ref, ssdg_ref, wout_ref, ng_ref, wq_ref,
                      kv_ref, wo_ref, o_ref, *, scale):
    d = h_ref.shape[1]
    hd = d // X_HEADS
    a = _rms(attn_ref[...], sbg_ref[...]).astype(BF16)
    s = _rms(ssd_ref[...], ssdg_ref[...]).astype(BF16)
    h = h_ref[...] + _dot(a, wout_ref[0:SB_WIDTH, :]) + _dot(s, wout_ref[SB_WIDTH:, :])
    u = _rms(h, ng_ref[...]).astype(BF16)
    q = _dot(u, wq_ref[...]).astype(BF16)
    heads = []
    for hh in range(X_HEADS):
        lanes = slice(hh * hd, (hh + 1) * hd)
        sc = _dot_nt(q[:, lanes], kv_ref[:, lanes]) * scale
        p = jnp.exp(sc - jnp.max(sc, axis=-1, keepdims=True))
        p = p / jnp.sum(p, axis=-1, keepdims=True)
        heads.append(_dot(p.astype(BF16), kv_ref[:, d + hh * hd:d + (hh + 1) * hd]).astype(BF16))
    o_ref[...] = h + _dot(jnp.concatenate(heads, axis=-1), wo_ref[...])


def _mix_cross(h3, attn3, ssd3, sb_g, ssd_g, w_out, norm_g, wq, kv3, wo, *, tm):
    b, s, d = h3.shape
    mem = kv3.shape[1]
    tm = min(tm, s)
    const = lambda shape: _resident(shape, lambda bi, ti: (0,) * len(shape))
    tile = lambda w: pl.BlockSpec((None, tm, w), lambda bi, ti: (bi, ti, 0))
    return pl.pallas_call(
        functools.partial(_mix_cross_kernel, scale=(d // X_HEADS) ** -0.5),
        grid=(b, s // tm),
        in_specs=[tile(d), tile(SB_WIDTH), tile(SSD_WIDTH), const((1, SB_WIDTH)), const((1, SSD_WIDTH)),
                  const((SB_WIDTH + SSD_WIDTH, d)), const((1, d)), const((d, d)),
                  pl.BlockSpec((None, mem, 2 * d), lambda bi, ti: (bi, 0, 0)), const((d, d))],
        out_specs=tile(d),
        out_shape=jax.ShapeDtypeStruct((b, s, d), F32),
        compiler_params=_params("parallel", "parallel"),
        name="mix_cross",
    )(h3, attn3, ssd3, sb_g, ssd_g, w_out, norm_g, wq, kv3, wo)


def _swiglu_hidden(x, w1_ref, w3_ref, act_ref):
    ff = w1_ref.shape[1]
    for c0 in range(0, ff, HIDDEN_CHUNK):
        sl = slice(c0, min(c0 + HIDDEN_CHUNK, ff))
        act_ref[:, sl] = (_silu(_dot(x, w1_ref[:, sl])) * _dot(x, w3_ref[:, sl])).astype(BF16)


def _ffn_kernel(h_ref, g_ref, w1_ref, w3_ref, w2_ref, o_ref, act_ref):
    h = h_ref[...]
    _swiglu_hidden(_rms(h, g_ref[...]).astype(BF16), w1_ref, w3_ref, act_ref)
    o_ref[...] = h + _dot(act_ref[...], w2_ref[...])


def _resident(shape, index_map):
    return pl.BlockSpec(shape, index_map, pipeline_mode=pl.Buffered(1))


def _ffn(h, g, w1, w3, w2, *, layer, tm):
    n, d = h.shape
    ff = w1.shape[2]
    return pl.pallas_call(
        _ffn_kernel,
        grid=(n // tm,),
        in_specs=[pl.BlockSpec((tm, d), lambda i: (i, 0)),
                  pl.BlockSpec((1, d), lambda i: (0, 0)),
                  _resident((None, d, ff), lambda i: (layer, 0, 0)),
                  _resident((None, d, ff), lambda i: (layer, 0, 0)),
                  _resident((None, ff, d), lambda i: (layer, 0, 0))],
        out_specs=pl.BlockSpec((tm, d), lambda i: (i, 0)),
        out_shape=jax.ShapeDtypeStruct((n, d), F32),
        scratch_shapes=[pltpu.VMEM((tm, ff), BF16)],
        compiler_params=_params("parallel"),
        name="ffn",
    )(h, g, w1, w3, w2)


def _store_row_tiles(ref, x):
    parts = x.shape[1] // LANES
    for j in range(parts):
        ref[pl.ds(j, x.shape[0], stride=parts), :] = x[:, j * LANES:(j + 1) * LANES]


def _load_row_tiles(ref, rows, parts):
    return jnp.concatenate([ref[pl.ds(j, rows, stride=parts), :] for j in range(parts)], axis=-1)


def _router_kernel(h_ref, g_ref, wr_ref, u_ref, meta_ref, cnt_ref, run_ref):
    i = pl.program_id(0)
    tm = h_ref.shape[0]

    @pl.when(i == 0)
    def _():
        run_ref[...] = jnp.zeros(run_ref.shape, F32)

    u = _rms(h_ref[...], g_ref[...])
    _store_row_tiles(u_ref, u)
    uh, ul = _split2(u)
    part = _dot(uh, wr_ref[...])
    logits = part[:, :LANES] + part[:, LANES:] + _dot(ul, wr_ref[:, :LANES])
    lane = lax.broadcasted_iota(jnp.int32, (tm, LANES), 1)
    logits = jnp.where(lane < N_EXPERTS, logits, -jnp.inf)
    m1 = jnp.max(logits, axis=-1, keepdims=True)
    e1 = jnp.min(jnp.where(logits == m1, lane, LANES), axis=-1, keepdims=True)
    rest = jnp.where(lane == e1, -jnp.inf, logits)
    m2 = jnp.max(rest, axis=-1, keepdims=True)
    e2 = jnp.min(jnp.where(rest == m2, lane, LANES), axis=-1, keepdims=True)
    t = jnp.exp(m2 - m1)
    g1 = 1.0 / (1.0 + t)
    g2 = t / (1.0 + t)
    oh1 = (lane == e1).astype(F32)
    oh2 = (lane == e2).astype(F32)
    both = oh1 + oh2
    rows = lax.broadcasted_iota(jnp.int32, (tm, tm), 0)
    cols = lax.broadcasted_iota(jnp.int32, (tm, tm), 1)
    earlier = (cols < rows).astype(BF16)
    before = _dot(earlier, both.astype(BF16)) + run_ref[...]
    r1 = jnp.sum(before * oh1, axis=-1, keepdims=True)
    r2 = jnp.sum(before * oh2, axis=-1, keepdims=True)
    meta = jnp.where(lane == 0, e1.astype(F32), 0.0)
    meta = jnp.where(lane == 1, e2.astype(F32), meta)
    meta = jnp.where(lane == 2, g1, meta)
    meta = jnp.where(lane == 3, g2, meta)
    meta = jnp.where(lane == 4, r1, meta)
    meta = jnp.where(lane == 5, r2, meta)
    meta_ref[...] = meta
    run_ref[...] += jnp.sum(both, axis=0, keepdims=True)
    cnt_ref[...] = run_ref[...]


def _router(h, g, wr, *, tm):
    n, d = h.shape
    parts = d // LANES
    return pl.pallas_call(
        _router_kernel,
        grid=(n // tm,),
        in_specs=[pl.BlockSpec((tm, d), lambda i: (i, 0)),
                  pl.BlockSpec((1, d), lambda i: (0, 0)),
                  pl.BlockSpec((d, 2 * LANES), lambda i: (0, 0))],
        out_specs=[pl.BlockSpec((tm * parts, LANES), lambda i: (i, 0)),
                   pl.BlockSpec((tm, LANES), lambda i: (i, 0)),
                   pl.BlockSpec((1, LANES), lambda i: (0, 0))],
        out_shape=[jax.ShapeDtypeStruct((n * parts, LANES), F32),
                   jax.ShapeDtypeStruct((n, LANES), F32),
                   jax.ShapeDtypeStruct((1, LANES), F32)],
        scratch_shapes=[pltpu.VMEM((1, LANES), F32)],
        compiler_params=_params("arbitrary"),
        name="router",
    )(h, g, wr)


def _tile_rows(first_row, nrows, parts):
    return pl.ds(pl.multiple_of(first_row * parts, parts), nrows * parts)


def _scatter_kernel(dest_ref, fill_ref, u_ref, xs_ref, zero_ref, ubuf_ref, in_sems, row_sems, fill_sem,
                    *, tm, parts):
    i = pl.program_id(0)
    last = pl.num_programs(0) - 1
    phase = i % 2

    def tile_in(step, slot):
        return pltpu.make_async_copy(u_ref.at[_tile_rows(step * tm, tm, parts)], ubuf_ref.at[slot],
                                     in_sems.at[slot])

    def wait_rows(slot):
        for k in range(2):
            pltpu.make_async_copy(ubuf_ref.at[slot], xs_ref.at[_tile_rows(0, tm, parts)],
                                  row_sems.at[slot, k]).wait()

    @pl.when(i == 0)
    def _():
        zero_ref[...] = jnp.zeros(zero_ref.shape, zero_ref.dtype)
        for e in range(fill_ref.shape[0]):
            fill = pltpu.make_async_copy(zero_ref, xs_ref.at[_tile_rows(fill_ref[e], MOE_TILE, parts)],
                                         fill_sem)
            fill.start()
            fill.wait()
        tile_in(0, 0).start()

    @pl.when(i > 0)
    def _():
        wait_rows(1 - phase)

    @pl.when(i < last)
    def _():
        tile_in(i + 1, 1 - phase).start()

    tile_in(i, phase).wait()

    def start(r, _):
        for k in range(2):
            pltpu.make_async_copy(ubuf_ref.at[phase, _tile_rows(r, 1, parts)],
                                  xs_ref.at[_tile_rows(dest_ref[0, 2 * r + k], 1, parts)],
                                  row_sems.at[phase, k]).start(priority=k)
        return 0

    lax.fori_loop(0, tm, start, 0, unroll=DMA_UNROLL)

    @pl.when(i == last)
    def _():
        wait_rows(phase)


def _dest_spec(tm):
    return pl.BlockSpec((None, 1, 2 * tm), lambda i: (i, 0, 0), memory_space=pltpu.SMEM)


def _scatter_rows(dest, fill_start, u, cap, *, tm, parts):
    return pl.pallas_call(
        functools.partial(_scatter_kernel, tm=tm, parts=parts),
        grid=(u.shape[0] // (tm * parts),),
        in_specs=[_dest_spec(tm),
                  pl.BlockSpec(memory_space=pltpu.SMEM),
                  pl.BlockSpec(memory_space=pl.ANY)],
        out_specs=pl.BlockSpec(memory_space=pl.ANY),
        out_shape=jax.ShapeDtypeStruct((cap * parts, LANES), u.dtype),
        scratch_shapes=[pltpu.VMEM((MOE_TILE * parts, LANES), u.dtype),
                        pltpu.VMEM((2, tm * parts, LANES), u.dtype),
                        pltpu.SemaphoreType.DMA((2,)), pltpu.SemaphoreType.DMA((2, 2)),
                        pltpu.SemaphoreType.DMA],
        compiler_params=_params("arbitrary"),
        name="moe_scatter",
    )(dest, fill_start, u)


def _expert_kernel(be_ref, used_ref, x_ref, w1_ref, w3_ref, w2_ref, o_ref, act_ref):
    del be_ref
    i = pl.program_id(0)
    tm = act_ref.shape[0]
    parts = x_ref.shape[0] // tm

    @pl.when(i < used_ref[0])
    def _():
        _swiglu_hidden(_load_row_tiles(x_ref, tm, parts).astype(BF16), w1_ref, w3_ref, act_ref)
        _store_row_tiles(o_ref, _dot(act_ref[...], w2_ref[...]))

    @pl.when(i >= used_ref[0])
    def _():
        o_ref[...] = jnp.zeros(o_ref.shape, F32)


def _experts(block_e, used, xs, w1, w3, w2, *, layer, tm):
    d, ff = w1.shape[2], w1.shape[3]
    parts = d // LANES
    blocks = xs.shape[0] // (tm * parts)
    return pl.pallas_call(
        _expert_kernel,
        grid_spec=pltpu.PrefetchScalarGridSpec(
            num_scalar_prefetch=2,
            grid=(blocks,),
            in_specs=[pl.BlockSpec((tm * parts, LANES), lambda i, be, nu: (jnp.minimum(i, nu[0] - 1), 0)),
                      _resident((None, None, d, ff), lambda i, be, nu: (layer, be[i], 0, 0)),
                      _resident((None, None, d, ff), lambda i, be, nu: (layer, be[i], 0, 0)),
                      _resident((None, None, ff, d), lambda i, be, nu: (layer, be[i], 0, 0))],
            out_specs=pl.BlockSpec((tm * parts, LANES), lambda i, be, nu: (i, 0)),
            scratch_shapes=[pltpu.VMEM((tm, ff), BF16)],
        ),
        out_shape=jax.ShapeDtypeStruct(xs.shape, F32),
        compiler_params=_params("arbitrary"),
        name="moe_experts",
    )(block_e, used, xs, w1, w3, w2)


def _combine_kernel(dest_ref, next_ref, h_ref, meta_ref, y_ref, fg_ref, o_ref, buf_ref, sems, *,
                    final_norm):
    tm, d = h_ref.shape
    parts = d // LANES
    i = pl.program_id(0)

    def gather(idx_ref, phase):
        def start(r, _):
            for k in range(2):
                pltpu.make_async_copy(y_ref.at[_tile_rows(idx_ref[0, 2 * r + k], 1, parts)],
                                      buf_ref.at[phase, k, _tile_rows(r, 1, parts)],
                                      sems.at[phase, k]).start(priority=k)
            return 0

        lax.fori_loop(0, tm, start, 0, unroll=DMA_UNROLL)

    @pl.when(i == 0)
    def _():
        gather(dest_ref, 0)

    @pl.when(i + 1 < pl.num_programs(0))
    def _():
        gather(next_ref, (i + 1) % 2)

    phase = i % 2
    for k in range(2):
        pltpu.make_async_copy(y_ref.at[_tile_rows(0, tm, parts)], buf_ref.at[phase, k],
                              sems.at[phase, k]).wait()
    meta = meta_ref[...]
    out = (h_ref[...] + meta[:, 2:3] * _load_row_tiles(buf_ref.at[phase, 0], tm, parts)
           + meta[:, 3:4] * _load_row_tiles(buf_ref.at[phase, 1], tm, parts))
    if final_norm:
        out = _rms(out, fg_ref[...])
    o_ref[...] = out


def _combine(dest, h, meta, y, fg, *, tm, final_norm):
    n, d = h.shape
    last = n // tm - 1
    next_spec = pl.BlockSpec((None, 1, 2 * tm), lambda i: (jnp.minimum(i + 1, last), 0, 0),
                             memory_space=pltpu.SMEM)
    return pl.pallas_call(
        functools.partial(_combine_kernel, final_norm=final_norm),
        grid=(n // tm,),
        in_specs=[_dest_spec(tm), next_spec,
                  pl.BlockSpec((tm, d), lambda i: (i, 0)),
                  pl.BlockSpec((tm, LANES), lambda i: (i, 0)),
                  pl.BlockSpec(memory_space=pl.ANY),
                  pl.BlockSpec((1, d), lambda i: (0, 0))],
        out_specs=pl.BlockSpec((tm, d), lambda i: (i, 0)),
        out_shape=jax.ShapeDtypeStruct((n, d), F32),
        scratch_shapes=[pltpu.VMEM((2, 2, tm * (d // LANES), LANES), F32),
                        pltpu.SemaphoreType.DMA((2, 2))],
        compiler_params=_params("arbitrary"),
        name="moe_combine",
    )(dest, dest, h, meta, y, fg)


def _final_norm_kernel(h_ref, g_ref, o_ref):
    o_ref[...] = _rms(h_ref[...], g_ref[...])


def _final_norm(h, g, *, tm):
    n, d = h.shape
    return pl.pallas_call(
        _final_norm_kernel,
        grid=(n // tm,),
        in_specs=[pl.BlockSpec((tm, d), lambda i: (i, 0)), pl.BlockSpec((1, d), lambda i: (0, 0))],
        out_specs=pl.BlockSpec((tm, d), lambda i: (i, 0)),
        out_shape=jax.ShapeDtypeStruct((n, d), F32),
        compiler_params=_params("parallel"),
        name="final_norm",
    )(h, g)


def _moe(h, g, router_w, w1, w3, w2, final_g, *, layer, final_norm):
    n, d = h.shape
    assert (d // LANES) % SUBLANES == 0, "row-tile layout needs whole (8, 128) tiles per token row"
    tm = MOE_TILE
    wr = jnp.zeros((d, LANES), F32).at[:, :N_EXPERTS].set(router_w.astype(F32))
    u, meta, counts = _router(h, g, jnp.concatenate(_split2(wr), axis=1), tm=ROW_TILE)
    counts = counts[0, :N_EXPERTS].astype(jnp.int32)
    padded = (counts + tm - 1) // tm * tm
    pends = jnp.cumsum(padded)
    pstarts = pends - padded
    experts = meta[:, 0:2].astype(jnp.int32)
    dest = (pstarts[experts] + meta[:, 4:6].astype(jnp.int32)).reshape(n // ROW_TILE, 1, 2 * ROW_TILE)
    cap = (-(-2 * n // tm)) * tm + N_EXPERTS * tm
    block_start = jnp.arange(cap // tm, dtype=jnp.int32) * tm
    block_e = jnp.minimum(jnp.sum((block_start[:, None] >= pends[None, :]).astype(jnp.int32), axis=1),
                          N_EXPERTS - 1)
    used = (pends[-1:] // tm).astype(jnp.int32)
    spare = jnp.minimum(used + jnp.arange(N_EXPERTS, dtype=jnp.int32), cap // tm - 1) * tm
    fill_start = jnp.concatenate([jnp.maximum(pends - tm, 0), spare])
    xs = _scatter_rows(dest, fill_start, u, cap, tm=ROW_TILE, parts=d // LANES)
    y = _experts(block_e, used, xs, w1, w3, w2, layer=layer, tm=tm)
    return _combine(dest, h, meta, y, final_g, tm=ROW_TILE, final_norm=final_norm)


def kernel(x, mem, mem_norm, norm_mix, w_in, conv_w, conv_b, dt_bias, a_log, d_skip, sb_norm, ssd_norm, w_out, norm_cross, wq_x, wk_x, wv_x, wo_x, norm_ffn, ffn_w1, ffn_w3, ffn_w2, router, moe_w1, moe_w3, moe_w2, final_norm):
    b, s, d = x.shape
    depth = w_in.shape[0]
    n = b * s
    mlen = mem.shape[1]
    main_cols = 3 * SB_WIDTH + SSD_WIDTH + conv_w.shape[2]
    row = lambda v: v.astype(F32)[None, :]
    h = x.astype(F32).reshape(n, d)
    mem2 = mem.astype(F32).reshape(b * mlen, d)
    dense_w = [w.astype(BF16) for w in (ffn_w1, ffn_w3, ffn_w2)]
    expert_w = [w.astype(BF16) for w in (moe_w1, moe_w3, moe_w2)]
    for i in range(depth):
        w_main = w_in[i, :, :main_cols].astype(BF16)
        w_dt = jnp.zeros((d, LANES), BF16).at[:, :SSD_HEADS].set(w_in[i, :, main_cols:].astype(BF16))
        proj, dt_raw = _norm_proj(h, row(norm_mix[i]), w_main, w_dt, tm=MATMUL_TILE)
        proj3 = proj.reshape(b, s, main_cols)
        attn = _sb_attention(proj3, blk=SB_BLOCK, heads_per_step=SB_HEADS_PER_STEP)
        ssd = _ssd(proj3, dt_raw.reshape(b, s, LANES), conv_w[i], conv_b[i], dt_bias[i], a_log[i],
                   d_skip[i], chunk=SSD_CHUNK)
        w_kv = jnp.concatenate([wk_x[i], wv_x[i]], axis=1).astype(BF16)
        kv = _norm_proj(mem2, row(mem_norm), w_kv, tm=MATMUL_TILE)
        h = _mix_cross(h.reshape(b, s, d), attn, ssd, row(sb_norm[i]), row(ssd_norm[i]),
                       w_out[i].astype(BF16), row(norm_cross[i]), wq_x[i].astype(BF16),
                       kv.reshape(b, mlen, 2 * d), wo_x[i].astype(BF16), tm=MATMUL_TILE).reshape(n, d)
        j = i // 2
        last = i == depth - 1
        if i % 2 == 0:
            h = _ffn(h, row(norm_ffn[i]), *dense_w, layer=j, tm=MATMUL_TILE)
            if last:
                h = _final_norm(h, row(final_norm), tm=ROW_TILE)
        else:
            h = _moe(h, row(norm_ffn[i]), router[j], *expert_w, row(final_norm), layer=j, final_norm=last)
    return h.reshape(b, s, d)
```
